```python
import jax
import jax.numpy as jnp
from jax import lax
import numpy as np

D_MODEL = 1024
BATCH = 16
SEQ = 2048
DEPTH = 2

CONV_WIDTH = 512
CONV_GROUPS = 8
CONV_K = 3
MLA_HEADS = 8
MLA_Q_LORA = 256
MLA_KV_LORA = 128
MLA_NOPE = 64
MLA_ROPE = 32
MLA_V = 64
MLA_QK = MLA_NOPE + MLA_ROPE
ROPE_THETA = 10000.0
DIL_PATTERNS = ((128, 1), (512, 4), (2048, 16))
DIL_GROUPS = len(DIL_PATTERNS)
DIL_HEADS = 8
DIL_HEAD_DIM = 64
DIL_WIDTH = DIL_HEADS * DIL_HEAD_DIM
N_BRANCH = 3
Q_BLOCK = 128
EPS = 1e-6

SPLIT_SIZES = ((CONV_WIDTH,) * 4
               + (MLA_Q_LORA, MLA_KV_LORA, MLA_ROPE, MLA_HEADS * MLA_V)
               + (DIL_GROUPS * DIL_WIDTH,) * 3 + (DIL_WIDTH,)
               + (N_BRANCH * D_MODEL,))
SPLIT_POINTS = tuple(int(v) for v in np.cumsum(SPLIT_SIZES)[:-1])
N_IN = int(sum(SPLIT_SIZES))

kernel_name = 'hybrid_gatedconv_mla_dilated_layer'


def rmsnorm(x, g):
    xf = x.astype(jnp.float32)
    y = xf * lax.rsqrt(jnp.mean(xf * xf, axis=-1, keepdims=True) + EPS)
    return (y * g.astype(jnp.float32)).astype(x.dtype)


def rope_tables(S):
    inv = ROPE_THETA ** (-jnp.arange(0, MLA_ROPE, 2, dtype=jnp.float32) / MLA_ROPE)
    ang = jnp.arange(S, dtype=jnp.float32)[:, None] * inv[None, :]
    return jnp.cos(ang), jnp.sin(ang)


def apply_rope(t, cos, sin):
    tf = t.astype(jnp.float32)
    t1, t2 = jnp.split(tf, 2, axis=-1)
    c = cos[None, :, None, :]
    s = sin[None, :, None, :]
    return jnp.concatenate([t1 * c - t2 * s, t2 * c + t1 * s], axis=-1).astype(t.dtype)


def alibi_slopes():
    n = DIL_GROUPS * DIL_HEADS
    m = 2.0 ** (-8.0 * jnp.arange(1, n + 1, dtype=jnp.float32) / n)
    return m.reshape(DIL_GROUPS, DIL_HEADS)


def causal_block_attention(q, k, v, scale):
    B, H, S, Dk = q.shape
    Dv = v.shape[-1]
    nqb = S // Q_BLOCK
    qb = q.reshape(B, H, nqb, Q_BLOCK, Dk).transpose(2, 0, 1, 3, 4)
    kf = k.astype(jnp.float32)
    vf = v.astype(jnp.float32)
    kpos = jnp.arange(S)

    def one_block(args):
        qblk, i = args
        s = jnp.einsum('bhqc,bhkc->bhqk', qblk.astype(jnp.float32), kf) * scale
        qpos = i * Q_BLOCK + jnp.arange(Q_BLOCK)
        s = jnp.where(kpos[None, :] <= qpos[:, None], s, -jnp.inf)
        p = jax.nn.softmax(s, axis=-1)
        return jnp.einsum('bhqk,bhkc->bhqc', p, vf)

    o = lax.map(one_block, (qb, jnp.arange(nqb)))
    return o.transpose(1, 2, 0, 3, 4).reshape(B, H, S, Dv)


def dilated_group_attention(q, k, v, slopes, dilation, n_back):
    B, S, H, hd = q.shape
    L = S // dilation
    nb = -(-L // Q_BLOCK)
    Lp = nb * Q_BLOCK

    def to_sub(t):
        return t.reshape(B, L, dilation, H, hd).transpose(0, 2, 3, 1, 4)

    qs = jnp.pad(to_sub(q).astype(jnp.float32), ((0, 0), (0, 0), (0, 0), (0, Lp - L), (0, 0)))
    qs = qs.reshape(B, dilation, H, nb, Q_BLOCK, hd)

    def windows(t):
        tp = jnp.pad(to_sub(t).astype(jnp.float32), ((0, 0), (0, 0), (0, 0), (Q_BLOCK, Lp - L), (0, 0)))
        tb = tp.reshape(B, dilation, H, nb + 1, Q_BLOCK, hd)
        return jnp.concatenate([tb[:, :, :, :-1], tb[:, :, :, 1:]], axis=4)

    kw = windows(k)
    vw = windows(v)
    qq = jnp.arange(Q_BLOCK)[:, None]
    kk = jnp.arange(2 * Q_BLOCK)[None, :]
    j = Q_BLOCK + qq - kk
    key_idx = (jnp.arange(nb)[:, None, None] - 1) * Q_BLOCK + kk[None]
    valid = (j >= 0) & (j <= n_back) & (key_idx >= 0)
    dist = (dilation * j).astype(jnp.float32)
    bias = -slopes.reshape(1, 1, H, 1, 1, 1) * dist
    scale = DIL_HEAD_DIM ** -0.5
    s = jnp.einsum('bdhnqc,bdhnkc->bdhnqk', qs, kw) * scale + bias
    s = jnp.where(valid, s, -jnp.inf)
    lse = jax.nn.logsumexp(s, axis=-1)
    p = jnp.exp(s - lse[..., None])
    o = jnp.einsum('bdhnqk,bdhnkc->bdhnqc', p, vw)
    o = o.reshape(B, dilation, H, Lp, hd)[:, :, :, :L]
    lse = lse.reshape(B, dilation, H, Lp)[:, :, :, :L]
    o = o.transpose(0, 3, 1, 2, 4).reshape(B, S, H, hd)
    lse = lse.transpose(0, 3, 1, 2).reshape(B, S, H)
    return o, lse


def hybrid_layer(x, norm_g, w_in, b_gate, conv_w, conv_b, q_a_norm_g, w_uq, kv_a_norm_g, w_ukv,
                 mla_q_norm_g, mla_k_norm_g, dil_q_norm_g, dil_k_norm_g,
                 w_out_a, w_out_b, w_out_c, w_o, cos, sin, slopes):
    B, S, _ = x.shape
    h = rmsnorm(x, norm_g)
    proj = h @ w_in
    (a_b, a_c, a_x, a_z, c_q, c_kv, k_pe, b_z, dq, dk, dv, c_z, gate_pre) = jnp.split(proj, SPLIT_POINTS, axis=-1)

    u = a_c * a_x
    up = jnp.pad(u, ((0, 0), (CONV_K - 1, 0), (0, 0)))
    conv = conv_b
    for tap in range(CONV_K):
        conv = conv + up[:, tap:tap + S] * conv_w[tap]
    y_a = a_b * conv * jax.nn.silu(a_z)

    q = (rmsnorm(c_q, q_a_norm_g) @ w_uq).reshape(B, S, MLA_HEADS, MLA_QK)
    kv = (rmsnorm(c_kv, kv_a_norm_g) @ w_ukv).reshape(B, S, MLA_HEADS, MLA_NOPE + MLA_V)
    k_nope, v = jnp.split(kv, [MLA_NOPE], axis=-1)
    k = jnp.concatenate([k_nope, jnp.broadcast_to(k_pe[:, :, None, :], (B, S, MLA_HEADS, MLA_ROPE))], axis=-1)
    q = rmsnorm(q, mla_q_norm_g)
    k = rmsnorm(k, mla_k_norm_g)
    q = jnp.concatenate([q[..., :MLA_NOPE], apply_rope(q[..., MLA_NOPE:], cos, sin)], axis=-1)
    k = jnp.concatenate([k[..., :MLA_NOPE], apply_rope(k[..., MLA_NOPE:], cos, sin)], axis=-1)
    o_b = causal_block_attention(q.transpose(0, 2, 1, 3), k.transpose(0, 2, 1, 3), v.transpose(0, 2, 1, 3),
                                 MLA_QK ** -0.5)
    o_b = o_b.transpose(0, 2, 1, 3).reshape(B, S, MLA_HEADS * MLA_V).astype(x.dtype)
    y_b = o_b * jax.nn.silu(b_z)

    dq = rmsnorm(dq.reshape(B, S, DIL_GROUPS, DIL_HEADS, DIL_HEAD_DIM), dil_q_norm_g[:, None, :])
    dk = rmsnorm(dk.reshape(B, S, DIL_GROUPS, DIL_HEADS, DIL_HEAD_DIM), dil_k_norm_g[:, None, :])
    dv = dv.reshape(B, S, DIL_GROUPS, DIL_HEADS, DIL_HEAD_DIM)
    outs = []
    lses = []
    for gi, (window, dilation) in enumerate(DIL_PATTERNS):
        o_g, lse_g = dilated_group_attention(dq[:, :, gi], dk[:, :, gi], dv[:, :, gi], slopes[gi],
                                             dilation, window // dilation)
        outs.append(o_g)
        lses.append(lse_g)
    alpha = jax.nn.softmax(jnp.stack(lses, axis=0), axis=0)
    o_c = jnp.sum(alpha[..., None] * jnp.stack(outs, axis=0), axis=0)
    o_c = o_c.reshape(B, S, DIL_WIDTH).astype(x.dtype)
    y_c = o_c * jax.nn.silu(c_z)

    g_a, g_b, g_c = jnp.split(jax.nn.sigmoid(gate_pre + b_gate), N_BRANCH, axis=-1)
    merged = g_a * (y_a @ w_out_a) + g_b * (y_b @ w_out_b) + g_c * (y_c @ w_out_c)
    return x + merged @ w_o


def setup_inputs(seed: int = 0) -> dict:
    key = jax.random.key(seed)
    ks = jax.random.split(key, 18)
    f32 = jnp.float32

    def nrm(k, shape, scale):
        return jax.random.normal(k, shape, f32) * scale

    def gain(k, shape):
        return 1.0 + 0.02 * jax.random.normal(k, shape, f32)

    Ld = DEPTH
    return {
        'x': jax.random.normal(ks[0], (BATCH, SEQ, D_MODEL), f32),
        'norm_g': gain(ks[1], (Ld, D_MODEL)),
        'w_in': nrm(ks[2], (Ld, D_MODEL, N_IN), D_MODEL ** -0.5),
        'b_gate': nrm(ks[3], (Ld, N_BRANCH * D_MODEL), 0.1),
        'conv_w': nrm(ks[4], (Ld, CONV_K, CONV_WIDTH), CONV_K ** -0.5),
        'conv_b': nrm(ks[5], (Ld, CONV_WIDTH), 0.02),
        'q_a_norm_g': gain(ks[6], (Ld, MLA_Q_LORA)),
        'w_uq': nrm(ks[7], (Ld, MLA_Q_LORA, MLA_HEADS * MLA_QK), MLA_Q_LORA ** -0.5),
        'kv_a_norm_g': gain(ks[8], (Ld, MLA_KV_LORA)),
        'w_ukv': nrm(ks[9], (Ld, MLA_KV_LORA, MLA_HEADS * (MLA_NOPE + MLA_V)), MLA_KV_LORA ** -0.5),
        'mla_q_norm_g': gain(ks[10], (Ld, MLA_QK)),
        'mla_k_norm_g': gain(ks[11], (Ld, MLA_QK)),
        'dil_q_norm_g': gain(ks[12], (Ld, DIL_GROUPS, DIL_HEAD_DIM)),
        'dil_k_norm_g': gain(ks[13], (Ld, DIL_GROUPS, DIL_HEAD_DIM)),
        'w_out_a': nrm(ks[14], (Ld, CONV_WIDTH, D_MODEL), CONV_WIDTH ** -0.5),
        'w_out_b': nrm(ks[15], (Ld, MLA_HEADS * MLA_V, D_MODEL), (MLA_HEADS * MLA_V) ** -0.5),
        'w_out_c': nrm(ks[16], (Ld, DIL_WIDTH, D_MODEL), DIL_WIDTH ** -0.5),
        'w_o': nrm(ks[17], (Ld, D_MODEL, D_MODEL), D_MODEL ** -0.5),
    }


def reference(x, norm_g, w_in, b_gate, conv_w, conv_b, q_a_norm_g, w_uq, kv_a_norm_g, w_ukv,
              mla_q_norm_g, mla_k_norm_g, dil_q_norm_g, dil_k_norm_g, w_out_a, w_out_b, w_out_c, w_o):
    cos, sin = rope_tables(x.shape[1])
    slopes = alibi_slopes()
    for l in range(DEPTH):
        x = hybrid_layer(x, norm_g[l], w_in[l], b_gate[l], conv_w[l], conv_b[l], q_a_norm_g[l], w_uq[l],
                         kv_a_norm_g[l], w_ukv[l], mla_q_norm_g[l], mla_k_norm_g[l], dil_q_norm_g[l],
                         dil_k_norm_g[l], w_out_a[l], w_out_b[l], w_out_c[l], w_o[l], cos, sin, slopes)
    return x
```

```python
import functools

import numpy as np
import jax
import jax.numpy as jnp
from jax import lax
from jax.experimental import pallas as pl
from jax.experimental.pallas import tpu as pltpu

D_MODEL = 1024
SEQ = 2048
DEPTH = 2
CONV_WIDTH = 512
CONV_K = 3
MLA_HEADS = 8
MLA_Q_LORA = 256
MLA_KV_LORA = 128
MLA_NOPE = 64
MLA_ROPE = 32
MLA_V = 64
MLA_QK = MLA_NOPE + MLA_ROPE
ROPE_THETA = 10000.0
DIL_PATTERNS = ((128, 1), (512, 4), (2048, 16))
DIL_GROUPS = len(DIL_PATTERNS)
DIL_HEADS = 8
DIL_HEAD_DIM = 64
DIL_WIDTH = DIL_HEADS * DIL_HEAD_DIM
N_BRANCH = 3
Q_BLOCK = 128
EPS = 1e-6

SPLIT_SIZES = ((CONV_WIDTH,) * 4
               + (MLA_Q_LORA, MLA_KV_LORA, MLA_ROPE, MLA_HEADS * MLA_V)
               + (DIL_GROUPS * DIL_WIDTH,) * 3 + (DIL_WIDTH,)
               + (N_BRANCH * D_MODEL,))
_OFFS = tuple(int(v) for v in np.cumsum((0,) + SPLIT_SIZES))
(_O_AB, _O_AC, _O_AX, _O_AZ, _O_CQ, _O_CKV, _O_KPE, _O_BZ, _O_DQ, _O_DK, _O_DV, _O_CZ, _O_GATE,
 _O_END) = _OFFS

LANES = 128
HEAD_PAD = LANES
HALF = LANES // 2
MXU_N = 256
NEG = -1e30

BF16 = jnp.bfloat16
F32 = jnp.float32

ROW_TILE = 256
MLA_TQ = 256
MLA_TK = 256
VMEM_LIMIT = 56 * 1024 * 1024


def _rms_scale(v, n):
    return lax.rsqrt(jnp.sum(v * v, axis=-1, keepdims=True) * (1.0 / n) + EPS)


def _sigmoid(v):
    return 1.0 / (1.0 + jnp.exp(-v))


def _dot(a, b):
    return jnp.dot(a, b, preferred_element_type=F32)


def _dot_t(a, b):
    return lax.dot_general(a, b, (((1,), (1,)), ((), ())), preferred_element_type=F32)


def _prep_kernel(x_ref, ng_ref, wlat_ref, gqa_ref, gkva_ref, wuq_ref, wuqr_ref, wk_ref, wv_ref,
                 vone_ref, gcq_ref, gsq_ref, gck_ref, gsk_ref, wd_ref, gdq_ref, gdk_ref, bd_ref,
                 q_ref, k_ref, v_ref, dq_ref, dk_ref, dv_ref):
    x = x_ref[...]
    h = (x * _rms_scale(x, D_MODEL) * ng_ref[...]).astype(BF16)

    lat = _dot(h, wlat_ref[...])
    cq = lat[:, :MLA_Q_LORA]
    ckv = lat[:, MLA_Q_LORA:MLA_Q_LORA + MLA_KV_LORA]
    kpe = lat[:, 384:512]
    kper = lat[:, 512:640]
    cqn = (cq * _rms_scale(cq, MLA_Q_LORA) * gqa_ref[...]).astype(BF16)
    ckvn = (ckv * _rms_scale(ckv, MLA_KV_LORA) * gkva_ref[...]).astype(BF16)

    qraw = _dot(cqn, wuq_ref[...])
    qrot = _dot(cqn, wuqr_ref[...])
    kn = _dot(ckvn, wk_ref[...])
    vp = _dot(ckvn, wv_ref[...]) + vone_ref[...]
    v_ref[...] = vp.astype(BF16)

    gcq = gcq_ref[...]
    gsq = gsq_ref[...]
    gck = gck_ref[...]
    krot = kper * gsk_ref[...]
    for hd in range(MLA_HEADS):
        sl = slice(HEAD_PAD * hd, HEAD_PAD * (hd + 1))
        qh = qraw[:, sl]
        q_ref[:, sl] = (_rms_scale(qh, MLA_QK) * (qh * gcq + qrot[:, sl] * gsq)).astype(BF16)
        kh = kn[:, sl] + kpe
        k_ref[:, sl] = (_rms_scale(kh, MLA_QK) * (kh * gck + krot)).astype(BF16)

    nd = DIL_GROUPS * DIL_WIDTH
    bd = bd_ref[...]
    for c in range(nd // MXU_N):
        sl = slice(MXU_N * c, MXU_N * (c + 1))
        for off, g_ref, o_ref in ((0, gdq_ref, dq_ref), (nd, gdk_ref, dk_ref)):
            t = _dot(h, wd_ref[:, off + MXU_N * c: off + MXU_N * (c + 1)])
            ssq = _dot((t * t).astype(BF16), bd)
            o_ref[:, sl] = (t * lax.rsqrt(ssq * (1.0 / DIL_HEAD_DIM) + EPS) * g_ref[:, sl]).astype(BF16)
        dv_ref[:, sl] = _dot(h, wd_ref[:, 2 * nd + MXU_N * c: 2 * nd + MXU_N * (c + 1)]).astype(BF16)


def _const_spec(shape):
    nd = len(shape)
    return pl.BlockSpec(shape, lambda *_: (0,) * nd)


def _prep_call(x2, pw, T):
    tm = ROW_TILE
    nst = SEQ // tm
    nd = DIL_GROUPS * DIL_WIDTH
    row = lambda n: pl.BlockSpec((tm, n), lambda i: (i, 0))
    tab = pl.BlockSpec((tm, LANES), lambda i: (i % nst, 0))
    consts = [pw['ng'], pw['wlat'], pw['gqa'], pw['gkva'], pw['wuq'], pw['wuqr'], pw['wk'], pw['wv'],
              pw['vone']]
    tabs = [pw['gcq'], pw['gsq'], pw['gck'], pw['gsk']]
    consts2 = [pw['wd'], pw['gdq'], pw['gdk'], pw['bd']]
    in_specs = ([row(D_MODEL)] + [_const_spec(a.shape) for a in consts] + [tab] * 4
                + [_const_spec(a.shape) for a in consts2])
    out_shape = [jax.ShapeDtypeStruct((T, MLA_HEADS * HEAD_PAD), BF16)] * 3 + \
                [jax.ShapeDtypeStruct((T, nd), BF16)] * 3
    out_specs = [row(MLA_HEADS * HEAD_PAD)] * 3 + [row(nd)] * 3
    return pl.pallas_call(
        _prep_kernel,
        grid=(T // tm,),
        in_specs=in_specs,
        out_specs=out_specs,
        out_shape=out_shape,
        compiler_params=pltpu.CompilerParams(dimension_semantics=("arbitrary",),
                                             vmem_limit_bytes=VMEM_LIMIT),
        name="prep",
    )(x2, *consts, *tabs, *consts2)


def _mla_kernel(q_ref, k_ref, v_ref, o_ref):
    i = pl.program_id(2)
    tq, tk = MLA_TQ, MLA_TK
    lane = lax.broadcasted_iota(jnp.int32, (1, LANES), 1)
    lo = lane < HALF
    row = lax.broadcasted_iota(jnp.int32, (tq, tk), 0)
    col = lax.broadcasted_iota(jnp.int32, (tq, tk), 1)
    causal = col <= row
    outs = []
    for hd in range(2):
        sl = slice(HEAD_PAD * hd, HEAD_PAD * (hd + 1))
        q = q_ref[:, sl]

        def step(j, carry, masked):
            m, acc = carry
            start = pl.multiple_of(j * tk, tk)
            kj = k_ref[pl.ds(start, tk), sl]
            vj = v_ref[pl.ds(start, tk), sl]
            s = _dot_t(q, kj)
            if masked:
                s = jnp.where(causal, s, NEG)
            m_new = jnp.maximum(m, jnp.max(s, axis=-1, keepdims=True))
            alpha = jnp.exp(m - m_new)
            p = jnp.exp(s - m_new).astype(BF16)
            return m_new, acc * alpha + _dot(p, vj)

        carry = (jnp.full((tq, 1), NEG, F32), jnp.zeros((tq, LANES), F32))
        carry = lax.fori_loop(0, i, functools.partial(step, masked=False), carry)
        _, acc = step(i, carry, True)
        outs.append(acc / pltpu.roll(acc, HALF, 1))
    o_ref[...] = jnp.where(lo, outs[0], outs[1]).astype(BF16)


def _mla_call(q, k, v, B):
    T = q.shape[0]
    nq = SEQ // MLA_TQ
    return pl.pallas_call(
        _mla_kernel,
        grid=(B, MLA_HEADS // 2, nq),
        in_specs=[pl.BlockSpec((MLA_TQ, 2 * HEAD_PAD), lambda b, hp, i: (b * nq + i, hp)),
                  pl.BlockSpec((SEQ, 2 * HEAD_PAD), lambda b, hp, i: (b, hp)),
                  pl.BlockSpec((SEQ, 2 * HEAD_PAD), lambda b, hp, i: (b, hp))],
        out_specs=pl.BlockSpec((MLA_TQ, LANES), lambda b, hp, i: (b * nq + i, hp)),
        out_shape=jax.ShapeDtypeStruct((T, MLA_HEADS * MLA_V), BF16),
        compiler_params=pltpu.CompilerParams(
            dimension_semantics=("arbitrary", "arbitrary", "arbitrary"),
            vmem_limit_bytes=VMEM_LIMIT),
        name="mla",
    )(q, k, v)


def _dil_kernel(bias_ref, q0_ref, k0_ref, v0_ref, q1_ref, k1_ref, v1_ref, q2_ref, k2_ref, v2_ref,
                o_ref, tmp_ref, qs_ref, ks_ref, vs_ref, acc_ref, m_ref):
    lane = lax.broadcasted_iota(jnp.int32, (1, LANES), 1)
    lo = lane < HALF
    ins = ((q0_ref, k0_ref, v0_ref), (q1_ref, k1_ref, v1_ref), (q2_ref, k2_ref, v2_ref))

    for g, (_, d) in enumerate(DIL_PATTERNS):
        L = SEQ // d
        nblk = L // Q_BLOCK
        if d == 1:
            qc, kc, vc = ins[g]
        else:
            for src, dst in zip(ins[g], (qs_ref, ks_ref, vs_ref)):
                tmp_ref[...] = src[...].astype(F32)
                for r in range(d):
                    dst[r * L:(r + 1) * L, :] = tmp_ref[pl.ds(r, L, stride=d), :].astype(BF16)
            qc, kc, vc = qs_ref, ks_ref, vs_ref

        for hd in range(2):
            hm = lo if hd == 0 else jnp.logical_not(lo)
            bias = bias_ref[g, hd]

            def block(r, n, first):
                base = pl.multiple_of(r * L + n * Q_BLOCK, Q_BLOCK)
                qb = qc[pl.ds(base, Q_BLOCK), :]
                qb = jnp.where(hm, qb, jnp.zeros_like(qb))
                if first:
                    kb = kc[pl.ds(base, Q_BLOCK), :]
                    vb = vc[pl.ds(base, Q_BLOCK), :]
                    s = _dot_t(qb, kb) + bias[:, Q_BLOCK:]
                else:
                    kbase = pl.multiple_of(base - Q_BLOCK, Q_BLOCK)
                    kb = kc[pl.ds(kbase, 2 * Q_BLOCK), :]
                    vb = vc[pl.ds(kbase, 2 * Q_BLOCK), :]
                    s = _dot_t(qb, kb) + bias
                m = jnp.max(s, axis=-1, keepdims=True)
                p = jnp.exp(s - m).astype(BF16)
                vm = jnp.where(hm, vb, jnp.ones_like(vb))
                acc = _dot(p, vm)
                start = n * (Q_BLOCK * d) + r
                if d == 1:
                    rows = pl.ds(pl.multiple_of(start, Q_BLOCK), Q_BLOCK)
                else:
                    rows = pl.ds(start, Q_BLOCK, stride=d)
                acc_ref[hd, g, rows, :] = acc
                m_ref[hd, g, rows, :] = jnp.broadcast_to(m, (Q_BLOCK, LANES))

            def class_body(r, _):
                block(r, 0, True)
                if nblk > 1:
                    def nbody(n, _):
                        block(r, n, False)
                        return 0
                    lax.fori_loop(1, nblk, nbody, 0)
                return 0

            if d == 1:
                class_body(0, 0)
            else:
                lax.fori_loop(0, d, class_body, 0)

    chunk = 256

    def merge(c, _):
        rows = pl.ds(pl.multiple_of(c * chunk, chunk), chunk)
        outs = []
        for hd in range(2):
            ms = [m_ref[hd, g, rows, :] for g in range(DIL_GROUPS)]
            mx = jnp.maximum(jnp.maximum(ms[0], ms[1]), ms[2])
            tot = jnp.zeros((chunk, LANES), F32)
            for g in range(DIL_GROUPS):
                tot = tot + jnp.exp(ms[g] - mx) * acc_ref[hd, g, rows, :]
            outs.append(tot / pltpu.roll(tot, HALF, 1))
        o_ref[rows, :] = jnp.where(lo, outs[0], outs[1]).astype(BF16)
        return 0

    lax.fori_loop(0, SEQ // chunk, merge, 0)


def _dil_call(bias, dq, dk, dv, B):
    T = dq.shape[0]
    nhp = DIL_HEADS // 2
    ncb = DIL_WIDTH // LANES

    def grp(g):
        return pl.BlockSpec((SEQ, LANES), lambda b, hp: (b, g * ncb + hp))

    in_specs = [pl.BlockSpec((DIL_GROUPS, 2, Q_BLOCK, 2 * Q_BLOCK), lambda b, hp: (0, hp, 0, 0))]
    args = [bias]
    for g in range(DIL_GROUPS):
        in_specs += [grp(g)] * 3
        args += [dq, dk, dv]
    return pl.pallas_call(
        _dil_kernel,
        grid=(B, nhp),
        in_specs=in_specs,
        out_specs=pl.BlockSpec((SEQ, LANES), lambda b, hp: (b, hp)),
        out_shape=jax.ShapeDtypeStruct((T, DIL_WIDTH), BF16),
        scratch_shapes=[pltpu.VMEM((SEQ, LANES), F32),
                        pltpu.VMEM((SEQ, LANES), BF16),
                        pltpu.VMEM((SEQ, LANES), BF16),
                        pltpu.VMEM((SEQ, LANES), BF16),
                        pltpu.VMEM((2, DIL_GROUPS, SEQ, LANES), F32),
                        pltpu.VMEM((2, DIL_GROUPS, SEQ, LANES), F32)],
        compiler_params=pltpu.CompilerParams(dimension_semantics=("arbitrary", "arbitrary"),
                                             vmem_limit_bytes=VMEM_LIMIT),
        name="dil",
    )(*args)


def _out_kernel(x_ref, ob_ref, oc_ref, ng_ref, wa_ref, wz_ref, wg_ref, bg_ref, cw_ref, cb_ref,
                woa_ref, wob_ref, woc_ref, wo_ref, out_ref, ubuf):
    tm = ROW_TILE
    i = pl.program_id(0)
    x = x_ref[...]
    h = (x * _rms_scale(x, D_MODEL) * ng_ref[...]).astype(BF16)

    @pl.when(i % (SEQ // tm) == 0)
    def _():
        ubuf[0:8, :] = jnp.zeros((8, CONV_WIDTH), F32)

    W = CONV_WIDTH
    u = _dot(h, wa_ref[:, W:2 * W]) * _dot(h, wa_ref[:, 2 * W:3 * W])
    ubuf[8:8 + tm, :] = u
    conv = (cb_ref[...] + ubuf[6:6 + tm, :] * cw_ref[0:1, :] + ubuf[7:7 + tm, :] * cw_ref[1:2, :]
            + u * cw_ref[2:3, :])
    ubuf[0:8, :] = u[tm - 8:tm, :]
    az = _dot(h, wa_ref[:, 3 * W:4 * W])
    ya = _dot(h, wa_ref[:, 0:W]) * conv * (az * _sigmoid(az))

    D = D_MODEL
    ga = _sigmoid(_dot(h, wg_ref[:, 0:D]) + bg_ref[:, 0:D])
    merged = ga * _dot(ya.astype(BF16), woa_ref[...])

    zb = _dot(h, wz_ref[:, 0:W])
    yb = ob_ref[...].astype(F32) * (zb * _sigmoid(zb))
    gb = _sigmoid(_dot(h, wg_ref[:, D:2 * D]) + bg_ref[:, D:2 * D])
    merged = merged + gb * _dot(yb.astype(BF16), wob_ref[...])

    zc = _dot(h, wz_ref[:, W:2 * W])
    yc = oc_ref[...].astype(F32) * (zc * _sigmoid(zc))
    gc = _sigmoid(_dot(h, wg_ref[:, 2 * D:3 * D]) + bg_ref[:, 2 * D:3 * D])
    merged = merged + gc * _dot(yc.astype(BF16), woc_ref[...])

    out_ref[...] = x + _dot(merged.astype(BF16), wo_ref[...])


def _out_call(x2, ob, oc, ow, T):
    tm = ROW_TILE
    row = lambda n: pl.BlockSpec((tm, n), lambda i: (i, 0))
    consts = [ow['ng'], ow['wa'], ow['wz'], ow['wg'], ow['bg'], ow['cw'], ow['cb'],
              ow['woa'], ow['wob'], ow['woc'], ow['wo']]
    return pl.pallas_call(
        _out_kernel,
        grid=(T // tm,),
        in_specs=[row(D_MODEL), row(MLA_HEADS * MLA_V), row(DIL_WIDTH)]
                 + [_const_spec(a.shape) for a in consts],
        out_specs=row(D_MODEL),
        out_shape=jax.ShapeDtypeStruct((T, D_MODEL), F32),
        scratch_shapes=[pltpu.VMEM((tm + 8, CONV_WIDTH), F32)],
        compiler_params=pltpu.CompilerParams(dimension_semantics=("arbitrary",),
                                             vmem_limit_bytes=VMEM_LIMIT),
        name="out",
    )(x2, ob, oc, *consts)


def _rope_tables():
    inv = ROPE_THETA ** (-jnp.arange(0, MLA_ROPE, 2, dtype=F32) / MLA_ROPE)
    ang = jnp.arange(SEQ, dtype=F32)[:, None] * inv[None, :]
    return jnp.cos(ang), jnp.sin(ang)


def _head_tables(g, cos, sin, scale):
    hr = MLA_ROPE // 2
    g1, g2 = g[MLA_NOPE:MLA_NOPE + hr], g[MLA_NOPE + hr:MLA_QK]
    ones = jnp.ones((SEQ, 1), F32)
    zpad = jnp.zeros((SEQ, HEAD_PAD - MLA_QK), F32)
    gc = jnp.concatenate([ones * g[None, :MLA_NOPE], cos * g1[None], cos * g2[None], zpad], axis=1)
    gs = jnp.concatenate([jnp.zeros((SEQ, MLA_NOPE), F32), sin * g2[None], sin * g1[None], zpad], axis=1)
    return gc * scale, gs * scale


def _rot_half_cols(w):
    hr = MLA_ROPE // 2
    return jnp.concatenate([-w[..., hr:], w[..., :hr]], axis=-1)


def _dil_bias():
    n = DIL_GROUPS * DIL_HEADS
    slopes = (2.0 ** (-8.0 * jnp.arange(1, n + 1, dtype=F32) / n)).reshape(DIL_GROUPS, DIL_HEADS)
    qq = jnp.arange(Q_BLOCK)[:, None]
    kk = jnp.arange(2 * Q_BLOCK)[None, :]
    j = Q_BLOCK + qq - kk
    tabs = []
    for gi, (window, d) in enumerate(DIL_PATTERNS):
        valid = (j >= 0) & (j <= window // d)
        dist = (d * j).astype(F32)
        tabs.append(jnp.where(valid[None], -slopes[gi][:, None, None] * dist[None], NEG))
    return jnp.stack(tabs, axis=0)


def _layer_params(l, p, cos, sin):
    w = p['w_in'][l]
    D = D_MODEL

    def cols(a, b):
        return w[:, a:b]

    zc = lambda n: jnp.zeros((D, n), F32)
    kpe_w = cols(_O_KPE, _O_BZ)
    wlat = jnp.concatenate([cols(_O_CQ, _O_CKV), cols(_O_CKV, _O_KPE),
                            zc(MLA_NOPE), kpe_w, zc(HEAD_PAD - MLA_QK),
                            zc(MLA_NOPE), _rot_half_cols(kpe_w), zc(HEAD_PAD - MLA_QK)], axis=1)

    wuq = p['w_uq'][l].reshape(MLA_Q_LORA, MLA_HEADS, MLA_QK)
    zq = jnp.zeros((MLA_Q_LORA, MLA_HEADS, HEAD_PAD - MLA_QK), F32)
    wuq_p = jnp.concatenate([wuq, zq], axis=-1).reshape(MLA_Q_LORA, MLA_HEADS * HEAD_PAD)
    wuq_r = jnp.concatenate([jnp.zeros((MLA_Q_LORA, MLA_HEADS, MLA_NOPE), F32),
                             _rot_half_cols(wuq[..., MLA_NOPE:]), zq], axis=-1)
    wuq_r = wuq_r.reshape(MLA_Q_LORA, MLA_HEADS * HEAD_PAD)

    wukv = p['w_ukv'][l].reshape(MLA_KV_LORA, MLA_HEADS, MLA_NOPE + MLA_V)
    zk = jnp.zeros((MLA_KV_LORA, MLA_HEADS, HALF), F32)
    wk = jnp.concatenate([wukv[..., :MLA_NOPE], zk], axis=-1).reshape(MLA_KV_LORA, MLA_HEADS * HEAD_PAD)
    wv4 = wukv[..., MLA_NOPE:].reshape(MLA_KV_LORA, MLA_HEADS // 2, 2, MLA_V)
    zv = jnp.zeros((MLA_KV_LORA, MLA_HEADS // 2, MLA_V), F32)
    wv = jnp.stack([jnp.concatenate([wv4[:, :, 0], zv], axis=-1),
                    jnp.concatenate([zv, wv4[:, :, 1]], axis=-1)], axis=2)
    wv = wv.reshape(MLA_KV_LORA, MLA_HEADS * HEAD_PAD)
    half_pat = jnp.concatenate([jnp.zeros((HALF,), F32), jnp.ones((HALF,), F32)])
    vone = jnp.concatenate([half_pat, 1.0 - half_pat] * (MLA_HEADS // 2))[None, :]

    gcq, gsq = _head_tables(p['mla_q_norm_g'][l], cos, sin, MLA_QK ** -0.5)
    gck, gsk = _head_tables(p['mla_k_norm_g'][l], cos, sin, 1.0)

    wd = jnp.concatenate([cols(_O_DQ, _O_DK), cols(_O_DK, _O_DV), cols(_O_DV, _O_CZ)], axis=1)
    gdq = (jnp.tile(p['dil_q_norm_g'][l][:, None, :], (1, DIL_HEADS, 1)).reshape(1, -1)
           * DIL_HEAD_DIM ** -0.5)
    gdk = jnp.tile(p['dil_k_norm_g'][l][:, None, :], (1, DIL_HEADS, 1)).reshape(1, -1)
    ii = jnp.arange(MXU_N) // DIL_HEAD_DIM
    bd = (ii[:, None] == ii[None, :]).astype(BF16)

    ng = p['norm_g'][l][None, :]
    prep = dict(ng=ng, wlat=wlat.astype(BF16), gqa=p['q_a_norm_g'][l][None, :],
                gkva=p['kv_a_norm_g'][l][None, :], wuq=wuq_p.astype(BF16), wuqr=wuq_r.astype(BF16),
                wk=wk.astype(BF16), wv=wv.astype(BF16), vone=vone, gcq=gcq, gsq=gsq, gck=gck, gsk=gsk,
                wd=wd.astype(BF16), gdq=gdq, gdk=gdk, bd=bd)
    out = dict(ng=ng, wa=cols(_O_AB, _O_CQ).astype(BF16),
               wz=jnp.concatenate([cols(_O_BZ, _O_DQ), cols(_O_CZ, _O_GATE)], axis=1).astype(BF16),
               wg=cols(_O_GATE, _O_END).astype(BF16), bg=p['b_gate'][l][None, :],
               cw=p['conv_w'][l], cb=p['conv_b'][l][None, :],
               woa=p['w_out_a'][l].astype(BF16), wob=p['w_out_b'][l].astype(BF16),
               woc=p['w_out_c'][l].astype(BF16), wo=p['w_o'][l].astype(BF16))
    return prep, out


def kernel(x, norm_g, w_in, b_gate, conv_w, conv_b, q_a_norm_g, w_uq, kv_a_norm_g, w_ukv, mla_q_norm_g,
           mla_k_norm_g, dil_q_norm_g, dil_k_norm_g, w_out_a, w_out_b, w_out_c, w_o):
    B, S, D = x.shape
    assert S == SEQ and D == D_MODEL
    T = B * S
    p = dict(norm_g=norm_g, w_in=w_in, b_gate=b_gate, conv_w=conv_w, conv_b=conv_b,
             q_a_norm_g=q_a_norm_g, w_uq=w_uq, kv_a_norm_g=kv_a_norm_g, w_ukv=w_ukv,
             mla_q_norm_g=mla_q_norm_g, mla_k_norm_g=mla_k_norm_g, dil_q_norm_g=dil_q_norm_g,
             dil_k_norm_g=dil_k_norm_g, w_out_a=w_out_a, w_out_b=w_out_b, w_out_c=w_out_c, w_o=w_o)
    cos, sin = _rope_tables()
    bias = _dil_bias()
    x2 = x.reshape(T, D)
    for l in range(DEPTH):
        pw, ow = _layer_params(l, p, cos, sin)
        q, k, v, dq, dk, dv = _prep_call(x2, pw, T)
        ob = _mla_call(q, k, v, B)
        oc = _dil_call(bias, dq, dk, dv, B)
        x2 = _out_call(x2, ob, oc, ow, T)
    return x2.reshape(B, S, D)
```

```python
import numpy as np
import jax
import jax.numpy as jnp
from jax import lax
from jax.experimental import pallas as pl
from jax.experimental.pallas import tpu as pltpu

D_MODEL = 1024
SEQ = 2048
DEPTH = 2
CONV_WIDTH = 512
CONV_K = 3
MLA_HEADS = 8
MLA_Q_LORA = 256
MLA_KV_LORA = 128
MLA_NOPE = 64
MLA_ROPE = 32
MLA_V = 64
MLA_QK = MLA_NOPE + MLA_ROPE
ROPE_THETA = 10000.0
DIL_PATTERNS = ((128, 1), (512, 4), (2048, 16))
DIL_GROUPS = len(DIL_PATTERNS)
DIL_HEADS = 8
DIL_HEAD_DIM = 64
DIL_WIDTH = DIL_HEADS * DIL_HEAD_DIM
N_BRANCH = 3
Q_BLOCK = 128
EPS = 1e-6

SPLIT_SIZES = ((CONV_WIDTH,) * 4
               + (MLA_Q_LORA, MLA_KV_LORA, MLA_ROPE, MLA_HEADS * MLA_V)
               + (DIL_GROUPS * DIL_WIDTH,) * 3 + (DIL_WIDTH,)
               + (N_BRANCH * D_MODEL,))
_OFFS = tuple(int(v) for v in np.cumsum((0,) + SPLIT_SIZES))
(_O_AB, _O_AC, _O_AX, _O_AZ, _O_CQ, _O_CKV, _O_KPE, _O_BZ, _O_DQ, _O_DK, _O_DV, _O_CZ, _O_GATE,
 _O_END) = _OFFS

LANES = 128
HEAD_PAD = LANES
HALF = LANES // 2
MXU_N = 256
NEG = -1e30

BF16 = jnp.bfloat16
F32 = jnp.float32

ROW_TILE = 256
MLA_T = 512
DIL_UNROLL = 3
VMEM_LIMIT = 56 * 1024 * 1024


def _rms_scale(v, n):
    return lax.rsqrt(jnp.sum(v * v, axis=-1, keepdims=True) * (1.0 / n) + EPS)


def _sigmoid(v):
    return 1.0 / (1.0 + jnp.exp(-v))


def _dot(a, b):
    return jnp.dot(a, b, preferred_element_type=F32)


def _dot_t(a, b):
    return lax.dot_general(a, b, (((1,), (1,)), ((), ())), preferred_element_type=F32)


def _prep_kernel(x_ref, ng_ref, wlat_ref, gqa_ref, gkva_ref, wuq_ref, wuqr_ref, wk_ref, wv_ref,
                 vone_ref, gcq_ref, gsq_ref, gck_ref, gsk_ref, wd_ref, gdq_ref, gdk_ref, bd_ref,
                 q_ref, k_ref, v_ref, dq_ref, dk_ref, dv_ref):
    x = x_ref[...]
    h = (x * _rms_scale(x, D_MODEL) * ng_ref[...]).astype(BF16)

    lat = _dot(h, wlat_ref[...])
    cq = lat[:, :MLA_Q_LORA]
    ckv = lat[:, MLA_Q_LORA:MLA_Q_LORA + MLA_KV_LORA]
    kpe = lat[:, 384:512]
    kper = lat[:, 512:640]
    cqn = (cq * _rms_scale(cq, MLA_Q_LORA) * gqa_ref[...]).astype(BF16)
    ckvn = (ckv * _rms_scale(ckv, MLA_KV_LORA) * gkva_ref[...]).astype(BF16)

    qraw = _dot(cqn, wuq_ref[...])
    qrot = _dot(cqn, wuqr_ref[...])
    kn = _dot(ckvn, wk_ref[...])
    vp = _dot(ckvn, wv_ref[...]) + vone_ref[...]
    v_ref[...] = vp.astype(BF16)

    gcq = gcq_ref[...]
    gsq = gsq_ref[...]
    gck = gck_ref[...]
    krot = kper * gsk_ref[...]
    for hd in range(MLA_HEADS):
        sl = slice(HEAD_PAD * hd, HEAD_PAD * (hd + 1))
        qh = qraw[:, sl]
        q_ref[:, sl] = (_rms_scale(qh, MLA_QK) * (qh * gcq + qrot[:, sl] * gsq)).astype(BF16)
        kh = kn[:, sl] + kpe
        k_ref[:, sl] = (_rms_scale(kh, MLA_QK) * (kh * gck + krot)).astype(BF16)

    nd = DIL_GROUPS * DIL_WIDTH
    bd = bd_ref[...]
    for c in range(nd // MXU_N):
        sl = slice(MXU_N * c, MXU_N * (c + 1))
        for off, g_ref, o_ref in ((0, gdq_ref, dq_ref), (nd, gdk_ref, dk_ref)):
            t = _dot(h, wd_ref[:, off + MXU_N * c: off + MXU_N * (c + 1)])
            ssq = _dot((t * t).astype(BF16), bd)
            o_ref[:, sl] = (t * lax.rsqrt(ssq * (1.0 / DIL_HEAD_DIM) + EPS) * g_ref[:, sl]).astype(BF16)
        dv_ref[:, sl] = _dot(h, wd_ref[:, 2 * nd + MXU_N * c: 2 * nd + MXU_N * (c + 1)]).astype(BF16)


def _const_spec(shape):
    nd = len(shape)
    return pl.BlockSpec(shape, lambda *_: (0,) * nd)


def _prep_call(x2, pw, T):
    tm = ROW_TILE
    nst = SEQ // tm
    nd = DIL_GROUPS * DIL_WIDTH
    row = lambda n: pl.BlockSpec((tm, n), lambda i: (i, 0))
    tab = pl.BlockSpec((tm, LANES), lambda i: (i % nst, 0))
    consts = [pw['ng'], pw['wlat'], pw['gqa'], pw['gkva'], pw['wuq'], pw['wuqr'], pw['wk'], pw['wv'],
              pw['vone']]
    tabs = [pw['gcq'], pw['gsq'], pw['gck'], pw['gsk']]
    consts2 = [pw['wd'], pw['gdq'], pw['gdk'], pw['bd']]
    in_specs = ([row(D_MODEL)] + [_const_spec(a.shape) for a in consts] + [tab] * 4
                + [_const_spec(a.shape) for a in consts2])
    out_shape = [jax.ShapeDtypeStruct((T, MLA_HEADS * HEAD_PAD), BF16)] * 3 + \
                [jax.ShapeDtypeStruct((T, nd), BF16)] * 3
    out_specs = [row(MLA_HEADS * HEAD_PAD)] * 3 + [row(nd)] * 3
    return pl.pallas_call(
        _prep_kernel,
        grid=(T // tm,),
        in_specs=in_specs,
        out_specs=out_specs,
        out_shape=out_shape,
        compiler_params=pltpu.CompilerParams(dimension_semantics=("arbitrary",),
                                             vmem_limit_bytes=VMEM_LIMIT),
        name="prep",
    )(x2, *consts, *tabs, *consts2)


def _mla_kernel(q_ref, k_ref, v_ref, o_ref, acc_ref, m_ref):
    i = pl.program_id(2)
    t = MLA_T
    hq = t // 2
    lane = lax.broadcasted_iota(jnp.int32, (1, LANES), 1)
    lo = lane < HALF
    acc_ref[...] = jnp.zeros(acc_ref.shape, F32)
    m_ref[...] = jnp.full(m_ref.shape, NEG, F32)

    def update(hd, rows, s, vj):
        m_old = m_ref[hd, rows, :]
        m_new = jnp.maximum(m_old, jnp.broadcast_to(jnp.max(s, axis=-1, keepdims=True), m_old.shape))
        alpha = jnp.exp(m_old - m_new)
        p = jnp.exp(s - jnp.concatenate([m_new] * (s.shape[1] // LANES), axis=1)).astype(BF16)
        acc_ref[hd, rows, :] = acc_ref[hd, rows, :] * alpha + _dot(p, vj)
        m_ref[hd, rows, :] = m_new

    def step(j, _):
        start = pl.multiple_of(j * t, t)
        for hd in range(2):
            sl = slice(HEAD_PAD * hd, HEAD_PAD * (hd + 1))
            s = _dot_t(q_ref[:, sl], k_ref[pl.ds(start, t), sl])
            update(hd, slice(0, t), s, v_ref[pl.ds(start, t), sl])
        return 0

    lax.fori_loop(0, i, step, 0)

    start = pl.multiple_of(i * t, t)
    top_mask = (lax.broadcasted_iota(jnp.int32, (hq, hq), 1)
                <= lax.broadcasted_iota(jnp.int32, (hq, hq), 0))
    bot_mask = (lax.broadcasted_iota(jnp.int32, (hq, t), 1)
                <= lax.broadcasted_iota(jnp.int32, (hq, t), 0) + hq)
    for hd in range(2):
        sl = slice(HEAD_PAD * hd, HEAD_PAD * (hd + 1))
        s = _dot_t(q_ref[0:hq, sl], k_ref[pl.ds(start, hq), sl])
        s = jnp.where(top_mask, s, NEG)
        update(hd, slice(0, hq), s, v_ref[pl.ds(start, hq), sl])
        s = _dot_t(q_ref[hq:t, sl], k_ref[pl.ds(start, t), sl])
        s = jnp.where(bot_mask, s, NEG)
        update(hd, slice(hq, t), s, v_ref[pl.ds(start, t), sl])

    outs = []
    for hd in range(2):
        acc = acc_ref[hd]
        outs.append(acc / pltpu.roll(acc, HALF, 1))
    o_ref[...] = jnp.where(lo, outs[0], outs[1]).astype(BF16)


def _mla_call(q, k, v, B):
    T = q.shape[0]
    nq = SEQ // MLA_T
    return pl.pallas_call(
        _mla_kernel,
        grid=(B, MLA_HEADS // 2, nq),
        in_specs=[pl.BlockSpec((MLA_T, 2 * HEAD_PAD), lambda b, hp, i: (b * nq + i, hp)),
                  pl.BlockSpec((SEQ, 2 * HEAD_PAD), lambda b, hp, i: (b, hp)),
                  pl.BlockSpec((SEQ, 2 * HEAD_PAD), lambda b, hp, i: (b, hp))],
        out_specs=pl.BlockSpec((MLA_T, LANES), lambda b, hp, i: (b * nq + i, hp)),
        out_shape=jax.ShapeDtypeStruct((T, MLA_HEADS * MLA_V), BF16),
        scratch_shapes=[pltpu.VMEM((2, MLA_T, LANES), F32),
                        pltpu.VMEM((2, MLA_T, LANES), F32)],
        compiler_params=pltpu.CompilerParams(
            dimension_semantics=("arbitrary", "arbitrary", "arbitrary"),
            vmem_limit_bytes=VMEM_LIMIT),
        name="mla",
    )(q, k, v)


def _dil_kernel(bias_ref, q0_ref, k0_ref, v0_ref, q1_ref, k1_ref, v1_ref, q2_ref, k2_ref, v2_ref,
                o_ref, tmp_ref, qst_ref, ks_ref, vs_ref, n_ref, l_ref, m_ref):
    lane = lax.broadcasted_iota(jnp.int32, (1, LANES), 1)
    lo = lane < HALF
    ins = ((q0_ref, k0_ref, v0_ref), (q1_ref, k1_ref, v1_ref), (q2_ref, k2_ref, v2_ref))
    Q = Q_BLOCK
    vs_ref[:, LANES:2 * LANES] = jnp.ones((SEQ, LANES), BF16)

    for g, (_, d) in enumerate(DIL_PATTERNS):
        L = SEQ // d
        nblk = L // Q

        for which, src in enumerate(ins[g]):
            if d == 1:
                pieces = [src[...]]
            else:
                tmp_ref[...] = src[...].astype(F32)
                pieces = [tmp_ref[pl.ds(r, L, stride=d), :].astype(BF16) for r in range(d)]
            for r, xr in enumerate(pieces):
                if which == 0:
                    zero = jnp.zeros_like(xr)
                    x0 = jnp.where(lo, xr, zero)
                    x1 = jnp.where(lo, zero, xr)
                    for n in range(nblk):
                        qst_ref[r * nblk + n, 0:Q, :] = x0[n * Q:(n + 1) * Q, :]
                        qst_ref[r * nblk + n, Q:2 * Q, :] = x1[n * Q:(n + 1) * Q, :]
                elif which == 1:
                    ks_ref[r * L:(r + 1) * L, :] = xr
                else:
                    vs_ref[r * L:(r + 1) * L, 0:LANES] = xr

        bias = bias_ref[g]

        def block(r, n, first):
            blk = r * nblk + n
            qb = qst_ref[blk]
            if first:
                kbase = pl.multiple_of(blk * Q, Q)
                nk = Q
                b = bias[:, Q:]
            else:
                kbase = pl.multiple_of(blk * Q - Q, Q)
                nk = 2 * Q
                b = bias
            s = _dot_t(qb, ks_ref[pl.ds(kbase, nk), :]) + b
            m = jnp.max(s, axis=-1, keepdims=True)
            p = jnp.exp(s - m).astype(BF16)
            acc = _dot(p, vs_ref[pl.ds(kbase, nk), :])
            mb = jnp.broadcast_to(m, (2 * Q, LANES))
            start = n * (Q * d) + r
            if d == 1:
                rows = pl.ds(pl.multiple_of(start, Q), Q)
            else:
                rows = pl.ds(start, Q, stride=d)
            n_ref[g, rows, :] = jnp.where(lo, acc[:Q, :LANES], acc[Q:, :LANES])
            l_ref[g, rows, :] = jnp.where(lo, acc[:Q, LANES:], acc[Q:, LANES:])
            m_ref[g, rows, :] = jnp.where(lo, mb[:Q], mb[Q:])

        if d == 1:
            block(0, 0, True)
        elif d <= 4:
            for r in range(d):
                block(r, 0, True)
        else:
            def first_body(it, _):
                for u in range(4):
                    block(it * 4 + u, 0, True)
                return 0
            lax.fori_loop(0, d // 4, first_body, 0)

        U = DIL_UNROLL
        if nblk > 1:
            assert (nblk - 1) % U == 0
            per_class = (nblk - 1) // U

            def full_body(it, _):
                r = it // per_class if d > 1 else 0
                n0 = 1 + (it % per_class) * U
                for u in range(U):
                    block(r, n0 + u, False)
                return 0
            lax.fori_loop(0, d * per_class, full_body, 0)

    chunk = 256

    def merge(c, _):
        rows = pl.ds(pl.multiple_of(c * chunk, chunk), chunk)
        ms = [m_ref[g, rows, :] for g in range(DIL_GROUPS)]
        mx = jnp.maximum(jnp.maximum(ms[0], ms[1]), ms[2])
        num = jnp.zeros((chunk, LANES), F32)
        den = jnp.zeros((chunk, LANES), F32)
        for g in range(DIL_GROUPS):
            w = jnp.exp(ms[g] - mx)
            num = num + w * n_ref[g, rows, :]
            den = den + w * l_ref[g, rows, :]
        o_ref[rows, :] = (num / den).astype(BF16)
        return 0

    lax.fori_loop(0, SEQ // chunk, merge, 0)


def _dil_call(bias, dq, dk, dv, B):
    T = dq.shape[0]
    nhp = DIL_HEADS // 2
    ncb = DIL_WIDTH // LANES

    def grp(g):
        return pl.BlockSpec((SEQ, LANES), lambda b, hp: (b, g * ncb + hp))

    in_specs = [pl.BlockSpec((DIL_GROUPS, None, 2 * Q_BLOCK, 2 * Q_BLOCK), lambda b, hp: (0, hp, 0, 0))]
    args = [bias]
    for g in range(DIL_GROUPS):
        in_specs += [grp(g)] * 3
        args += [dq, dk, dv]
    return pl.pallas_call(
        _dil_kernel,
        grid=(B, nhp),
        in_specs=in_specs,
        out_specs=pl.BlockSpec((SEQ, LANES), lambda b, hp: (b, hp)),
        out_shape=jax.ShapeDtypeStruct((T, DIL_WIDTH), BF16),
        scratch_shapes=[pltpu.VMEM((SEQ, LANES), F32),
                        pltpu.VMEM((SEQ // Q_BLOCK, 2 * Q_BLOCK, LANES), BF16),
                        pltpu.VMEM((SEQ, LANES), BF16),
                        pltpu.VMEM((SEQ, 2 * LANES), BF16),
                        pltpu.VMEM((DIL_GROUPS, SEQ, LANES), F32),
                        pltpu.VMEM((DIL_GROUPS, SEQ, LANES), F32),
                        pltpu.VMEM((DIL_GROUPS, SEQ, LANES), F32)],
        compiler_params=pltpu.CompilerParams(dimension_semantics=("arbitrary", "arbitrary"),
                                             vmem_limit_bytes=VMEM_LIMIT),
        name="dil",
    )(*args)


def _out_kernel(x_ref, ob_ref, oc_ref, ng_ref, wa_ref, wz_ref, wg_ref, bg_ref, cw_ref, cb_ref,
                woa_ref, wob_ref, woc_ref, wo_ref, out_ref, ubuf):
    tm = ROW_TILE
    i = pl.program_id(0)
    x = x_ref[...]
    h = (x * _rms_scale(x, D_MODEL) * ng_ref[...]).astype(BF16)

    @pl.when(i % (SEQ // tm) == 0)
    def _():
        ubuf[0:8, :] = jnp.zeros((8, CONV_WIDTH), F32)

    W = CONV_WIDTH
    u = _dot(h, wa_ref[:, W:2 * W]) * _dot(h, wa_ref[:, 2 * W:3 * W])
    ubuf[8:8 + tm, :] = u
    conv = (cb_ref[...] + ubuf[6:6 + tm, :] * cw_ref[0:1, :] + ubuf[7:7 + tm, :] * cw_ref[1:2, :]
            + u * cw_ref[2:3, :])
    ubuf[0:8, :] = u[tm - 8:tm, :]
    az = _dot(h, wa_ref[:, 3 * W:4 * W])
    ya = _dot(h, wa_ref[:, 0:W]) * conv * (az * _sigmoid(az))

    D = D_MODEL
    ga = _sigmoid(_dot(h, wg_ref[:, 0:D]) + bg_ref[:, 0:D])
    merged = ga * _dot(ya.astype(BF16), woa_ref[...])

    zb = _dot(h, wz_ref[:, 0:W])
    yb = ob_ref[...].astype(F32) * (zb * _sigmoid(zb))
    gb = _sigmoid(_dot(h, wg_ref[:, D:2 * D]) + bg_ref[:, D:2 * D])
    merged = merged + gb * _dot(yb.astype(BF16), wob_ref[...])

    zc = _dot(h, wz_ref[:, W:2 * W])
    yc = oc_ref[...].astype(F32) * (zc * _sigmoid(zc))
    gc = _sigmoid(_dot(h, wg_ref[:, 2 * D:3 * D]) + bg_ref[:, 2 * D:3 * D])
    merged = merged + gc * _dot(yc.astype(BF16), woc_ref[...])

    out_ref[...] = x + _dot(merged.astype(BF16), wo_ref[...])


def _out_call(x2, ob, oc, ow, T):
    tm = ROW_TILE
    row = lambda n: pl.BlockSpec((tm, n), lambda i: (i, 0))
    consts = [ow['ng'], ow['wa'], ow['wz'], ow['wg'], ow['bg'], ow['cw'], ow['cb'],
              ow['woa'], ow['wob'], ow['woc'], ow['wo']]
    return pl.pallas_call(
        _out_kernel,
        grid=(T // tm,),
        in_specs=[row(D_MODEL), row(MLA_HEADS * MLA_V), row(DIL_WIDTH)]
                 + [_const_spec(a.shape) for a in consts],
        out_specs=row(D_MODEL),
        out_shape=jax.ShapeDtypeStruct((T, D_MODEL), F32),
        scratch_shapes=[pltpu.VMEM((tm + 8, CONV_WIDTH), F32)],
        compiler_params=pltpu.CompilerParams(dimension_semantics=("arbitrary",),
                                             vmem_limit_bytes=VMEM_LIMIT),
        name="out",
    )(x2, ob, oc, *consts)


def _rope_tables():
    inv = ROPE_THETA ** (-jnp.arange(0, MLA_ROPE, 2, dtype=F32) / MLA_ROPE)
    ang = jnp.arange(SEQ, dtype=F32)[:, None] * inv[None, :]
    return jnp.cos(ang), jnp.sin(ang)


def _head_tables(g, cos, sin, scale):
    hr = MLA_ROPE // 2
    g1, g2 = g[MLA_NOPE:MLA_NOPE + hr], g[MLA_NOPE + hr:MLA_QK]
    ones = jnp.ones((SEQ, 1), F32)
    zpad = jnp.zeros((SEQ, HEAD_PAD - MLA_QK), F32)
    gc = jnp.concatenate([ones * g[None, :MLA_NOPE], cos * g1[None], cos * g2[None], zpad], axis=1)
    gs = jnp.concatenate([jnp.zeros((SEQ, MLA_NOPE), F32), sin * g2[None], sin * g1[None], zpad], axis=1)
    return gc * scale, gs * scale


def _rot_half_cols(w):
    hr = MLA_ROPE // 2
    return jnp.concatenate([-w[..., hr:], w[..., :hr]], axis=-1)


def _dil_bias():
    n = DIL_GROUPS * DIL_HEADS
    slopes = (2.0 ** (-8.0 * jnp.arange(1, n + 1, dtype=F32) / n)).reshape(DIL_GROUPS, DIL_HEADS)
    qq = jnp.arange(Q_BLOCK)[:, None]
    kk = jnp.arange(2 * Q_BLOCK)[None, :]
    j = Q_BLOCK + qq - kk
    tabs = []
    for gi, (window, d) in enumerate(DIL_PATTERNS):
        valid = (j >= 0) & (j <= window // d)
        dist = (d * j).astype(F32)
        tabs.append(jnp.where(valid[None], -slopes[gi][:, None, None] * dist[None], NEG))
    tab = jnp.stack(tabs, axis=0)
    return tab.reshape(DIL_GROUPS, DIL_HEADS // 2, 2 * Q_BLOCK, 2 * Q_BLOCK)


def _layer_params(l, p, cos, sin):
    w = p['w_in'][l]
    D = D_MODEL

    def cols(a, b):
        return w[:, a:b]

    zc = lambda n: jnp.zeros((D, n), F32)
    kpe_w = cols(_O_KPE, _O_BZ)
    wlat = jnp.concatenate([cols(_O_CQ, _O_CKV), cols(_O_CKV, _O_KPE),
                            zc(MLA_NOPE), kpe_w, zc(HEAD_PAD - MLA_QK),
                            zc(MLA_NOPE), _rot_half_cols(kpe_w), zc(HEAD_PAD - MLA_QK)], axis=1)

    wuq = p['w_uq'][l].reshape(MLA_Q_LORA, MLA_HEADS, MLA_QK)
    zq = jnp.zeros((MLA_Q_LORA, MLA_HEADS, HEAD_PAD - MLA_QK), F32)
    wuq_p = jnp.concatenate([wuq, zq], axis=-1).reshape(MLA_Q_LORA, MLA_HEADS * HEAD_PAD)
    wuq_r = jnp.concatenate([jnp.zeros((MLA_Q_LORA, MLA_HEADS, MLA_NOPE), F32),
                             _rot_half_cols(wuq[..., MLA_NOPE:]), zq], axis=-1)
    wuq_r = wuq_r.reshape(MLA_Q_LORA, MLA_HEADS * HEAD_PAD)

    wukv = p['w_ukv'][l].reshape(MLA_KV_LORA, MLA_HEADS, MLA_NOPE + MLA_V)
    zk = jnp.zeros((MLA_KV_LORA, MLA_HEADS, HALF), F32)
    wk = jnp.concatenate([wukv[..., :MLA_NOPE], zk], axis=-1).reshape(MLA_KV_LORA, MLA_HEADS * HEAD_PAD)
    wv4 = wukv[..., MLA_NOPE:].reshape(MLA_KV_LORA, MLA_HEADS // 2, 2, MLA_V)
    zv = jnp.zeros((MLA_KV_LORA, MLA_HEADS // 2, MLA_V), F32)
    wv = jnp.stack([jnp.concatenate([wv4[:, :, 0], zv], axis=-1),
                    jnp.concatenate([zv, wv4[:, :, 1]], axis=-1)], axis=2)
    wv = wv.reshape(MLA_KV_LORA, MLA_HEADS * HEAD_PAD)
    half_pat = jnp.concatenate([jnp.zeros((HALF,), F32), jnp.ones((HALF,), F32)])
    vone = jnp.concatenate([half_pat, 1.0 - half_pat] * (MLA_HEADS // 2))[None, :]

    gcq, gsq = _head_tables(p['mla_q_norm_g'][l], cos, sin, MLA_QK ** -0.5)
    gck, gsk = _head_tables(p['mla_k_norm_g'][l], cos, sin, 1.0)

    wd = jnp.concatenate([cols(_O_DQ, _O_DK), cols(_O_DK, _O_DV), cols(_O_DV, _O_CZ)], axis=1)
    gdq = (jnp.tile(p['dil_q_norm_g'][l][:, None, :], (1, DIL_HEADS, 1)).reshape(1, -1)
           * DIL_HEAD_DIM ** -0.5)
    gdk = jnp.tile(p['dil_k_norm_g'][l][:, None, :], (1, DIL_HEADS, 1)).reshape(1, -1)
    ii = jnp.arange(MXU_N) // DIL_HEAD_DIM
    bd = (ii[:, None] == ii[None, :]).astype(BF16)

    ng = p['norm_g'][l][None, :]
    prep = dict(ng=ng, wlat=wlat.astype(BF16), gqa=p['q_a_norm_g'][l][None, :],
                gkva=p['kv_a_norm_g'][l][None, :], wuq=wuq_p.astype(BF16), wuqr=wuq_r.astype(BF16),
                wk=wk.astype(BF16), wv=wv.astype(BF16), vone=vone, gcq=gcq, gsq=gsq, gck=gck, gsk=gsk,
                wd=wd.astype(BF16), gdq=gdq, gdk=gdk, bd=bd)
    out = dict(ng=ng, wa=cols(_O_AB, _O_CQ).astype(BF16),
               wz=jnp.concatenate([cols(_O_BZ, _O_DQ), cols(_O_CZ, _O_GATE)], axis=1).astype(BF16),
               wg=cols(_O_GATE, _O_END).astype(BF16), bg=p['b_gate'][l][None, :],
               cw=p['conv_w'][l], cb=p['conv_b'][l][None, :],
               woa=p['w_out_a'][l].astype(BF16), wob=p['w_out_b'][l].astype(BF16),
               woc=p['w_out_c'][l].astype(BF16), wo=p['w_o'][l].astype(BF16))
    return prep, out


def kernel(x, norm_g, w_in, b_gate, conv_w, conv_b, q_a_norm_g, w_uq, kv_a_norm_g, w_ukv, mla_q_norm_g,
           mla_k_norm_g, dil_q_norm_g, dil_k_norm_g, w_out_a, w_out_b, w_out_c, w_o):
    B, S, D = x.shape
    assert S == SEQ and D == D_MODEL
    T = B * S
    p = dict(norm_g=norm_g, w_in=w_in, b_gate=b_gate, conv_w=conv_w, conv_b=conv_b,
             q_a_norm_g=q_a_norm_g, w_uq=w_uq, kv_a_norm_g=kv_a_norm_g, w_ukv=w_ukv,
             mla_q_norm_g=mla_q_norm_g, mla_k_norm_g=mla_k_norm_g, dil_q_norm_g=dil_q_norm_g,
             dil_k_norm_g=dil_k_norm_g, w_out_a=w_out_a, w_out_b=w_out_b, w_out_c=w_out_c, w_o=w_o)
    cos, sin = _rope_tables()
    bias = _dil_bias()
    x2 = x.reshape(T, D)
    for l in range(DEPTH):
        pw, ow = _layer_params(l, p, cos, sin)
        q, k, v, dq, dk, dv = _prep_call(x2, pw, T)
        ob = _mla_call(q, k, v, B)
        oc = _dil_call(bias, dq, dk, dv, B)
        x2 = _out_call(x2, ob, oc, ow, T)
    return x2.reshape(B, S, D)
```

```python
import numpy as np
import jax
import jax.numpy as jnp
from jax import lax
from jax.experimental import pallas as pl
from jax.experimental.pallas import tpu as pltpu

D_MODEL = 1024
SEQ = 2048
DEPTH = 2
CONV_WIDTH = 512
CONV_K = 3
MLA_HEADS = 8
MLA_Q_LORA = 256
MLA_KV_LORA = 128
MLA_NOPE = 64
MLA_ROPE = 32
MLA_V = 64
MLA_QK = MLA_NOPE + MLA_ROPE
ROPE_THETA = 10000.0
DIL_PATTERNS = ((128, 1), (512, 4), (2048, 16))
DIL_GROUPS = len(DIL_PATTERNS)
DIL_HEADS = 8
DIL_HEAD_DIM = 64
DIL_WIDTH = DIL_HEADS * DIL_HEAD_DIM
N_BRANCH = 3
Q_BLOCK = 128
EPS = 1e-6

SPLIT_SIZES = ((CONV_WIDTH,) * 4
               + (MLA_Q_LORA, MLA_KV_LORA, MLA_ROPE, MLA_HEADS * MLA_V)
               + (DIL_GROUPS * DIL_WIDTH,) * 3 + (DIL_WIDTH,)
               + (N_BRANCH * D_MODEL,))
_OFFS = tuple(int(v) for v in np.cumsum((0,) + SPLIT_SIZES))
(_O_AB, _O_AC, _O_AX, _O_AZ, _O_CQ, _O_CKV, _O_KPE, _O_BZ, _O_DQ, _O_DK, _O_DV, _O_CZ, _O_GATE,
 _O_END) = _OFFS

LANES = 128
HEAD_PAD = LANES
HALF = LANES // 2
MXU_N = 256
NEG = -1e30

BF16 = jnp.bfloat16
F32 = jnp.float32

ROW_TILE = 512
MLA_T = 512
N_DSCR = 4
VMEM_LIMIT = 58 * 1024 * 1024


def _rms_scale(v, n):
    return lax.rsqrt(jnp.sum(v * v, axis=-1, keepdims=True) * (1.0 / n) + EPS)


def _sigmoid(v):
    return 1.0 / (1.0 + jnp.exp(-v))


def _dot(a, b):
    return jnp.dot(a, b, preferred_element_type=F32)


def _dot_t(a, b):
    return lax.dot_general(a, b, (((1,), (1,)), ((), ())), preferred_element_type=F32)


def _const_spec(shape):
    nd = len(shape)
    return pl.BlockSpec(shape, lambda *_: (0,) * nd, pipeline_mode=pl.Buffered(1))


def _prep_kernel(x_ref, ng_ref, wlat_ref, gqa_ref, gkva_ref, wuq_ref, wuqr_ref, wk_ref, wv_ref,
                 vone_ref, gcq_ref, gsq_ref, gck_ref, gsk_ref, wd_ref, gdq_ref, gdk_ref, bd_ref,
                 q_ref, k_ref, v_ref,
                 dq0_ref, dk0_ref, dv0_ref, dq1_ref, dk1_ref, dv1_ref, dq2_ref, dk2_ref, dv2_ref,
                 dscr):
    tm = ROW_TILE
    x = x_ref[...]
    h = (x * _rms_scale(x, D_MODEL) * ng_ref[...]).astype(BF16)

    lat = _dot(h, wlat_ref[...])
    cq = lat[:, :MLA_Q_LORA]
    ckv = lat[:, MLA_Q_LORA:MLA_Q_LORA + MLA_KV_LORA]
    kpe = lat[:, 384:512]
    kper = lat[:, 512:640]
    cqn = (cq * _rms_scale(cq, MLA_Q_LORA) * gqa_ref[...]).astype(BF16)
    ckvn = (ckv * _rms_scale(ckv, MLA_KV_LORA) * gkva_ref[...]).astype(BF16)

    qraw = _dot(cqn, wuq_ref[...])
    qrot = _dot(cqn, wuqr_ref[...])
    kn = _dot(ckvn, wk_ref[...])
    vp = _dot(ckvn, wv_ref[...]) + vone_ref[...]
    v_ref[...] = vp.astype(BF16)

    gcq = gcq_ref[...]
    gsq = gsq_ref[...]
    gck = gck_ref[...]
    krot = kper * gsk_ref[...]
    for hd in range(MLA_HEADS):
        sl = slice(HEAD_PAD * hd, HEAD_PAD * (hd + 1))
        qh = qraw[:, sl]
        q_ref[:, sl] = (_rms_scale(qh, MLA_QK) * (qh * gcq + qrot[:, sl] * gsq)).astype(BF16)
        kh = kn[:, sl] + kpe
        k_ref[:, sl] = (_rms_scale(kh, MLA_QK) * (kh * gck + krot)).astype(BF16)

    nd = DIL_GROUPS * DIL_WIDTH
    bd = bd_ref[...]
    outs = ((dq0_ref, dk0_ref, dv0_ref), (dq1_ref, dk1_ref, dv1_ref), (dq2_ref, dk2_ref, dv2_ref))
    n_store = [0]

    def store(kind, c, val):
        g, half = divmod(c, DIL_WIDTH // MXU_N)
        sl = slice(MXU_N * half, MXU_N * (half + 1))
        d = DIL_PATTERNS[g][1]
        ref = outs[g][kind]
        if d == 1:
            ref[:, sl] = val.astype(BF16)
        else:
            slot = n_store[0] % N_DSCR
            n_store[0] += 1
            for cb in range(MXU_N // LANES):
                dscr[slot, cb] = val[:, LANES * cb:LANES * (cb + 1)]
            for r in range(d):
                for cb in range(MXU_N // LANES):
                    c0 = MXU_N * half + LANES * cb
                    ref[r, :, c0:c0 + LANES] = dscr[slot, cb, pl.ds(r, tm // d, stride=d), :].astype(BF16)

    def finish(kind, c, t):
        g_ref = gdq_ref if kind == 0 else gdk_ref
        sl = slice(MXU_N * c, MXU_N * (c + 1))
        ssq = _dot((t * t).astype(BF16), bd)
        store(kind, c, t * lax.rsqrt(ssq * (1.0 / DIL_HEAD_DIM) + EPS) * g_ref[:, sl])

    pending = None
    nch = nd // MXU_N
    for kind in range(2):
        for c in range(nch):
            col = kind * nd + MXU_N * c
            t = _dot(h, wd_ref[:, col:col + MXU_N])
            if pending is not None:
                finish(*pending)
            pending = (kind, c, t)
            if kind == 1:
                store(2, c, _dot(h, wd_ref[:, 2 * nd + MXU_N * c: 2 * nd + MXU_N * (c + 1)]))
    finish(*pending)


def _prep_call(x2, pw, B):
    T = x2.shape[0]
    tm = ROW_TILE
    nst = SEQ // tm
    row = lambda n: pl.BlockSpec((tm, n), lambda i: (i, 0))
    tab = pl.BlockSpec((tm, LANES), lambda i: (i % nst, 0))
    consts = [pw['ng'], pw['wlat'], pw['gqa'], pw['gkva'], pw['wuq'], pw['wuqr'], pw['wk'], pw['wv'],
              pw['vone']]
    tabs = [pw['gcq'], pw['gsq'], pw['gck'], pw['gsk']]
    consts2 = [pw['wd'], pw['gdq'], pw['gdk'], pw['bd']]
    in_specs = ([row(D_MODEL)] + [_const_spec(a.shape) for a in consts] + [tab] * 4
                + [_const_spec(a.shape) for a in consts2])
    out_shape = [jax.ShapeDtypeStruct((T, MLA_HEADS * HEAD_PAD), BF16)] * 3
    out_specs = [row(MLA_HEADS * HEAD_PAD)] * 3
    for _, d in DIL_PATTERNS:
        if d == 1:
            out_shape += [jax.ShapeDtypeStruct((T, DIL_WIDTH), BF16)] * 3
            out_specs += [row(DIL_WIDTH)] * 3
        else:
            out_shape += [jax.ShapeDtypeStruct((B, d, SEQ // d, DIL_WIDTH), BF16)] * 3
            out_specs += [pl.BlockSpec((None, d, tm // d, DIL_WIDTH),
                                       lambda i: (i // nst, 0, i % nst, 0))] * 3
    return pl.pallas_call(
        _prep_kernel,
        grid=(T // tm,),
        in_specs=in_specs,
        out_specs=out_specs,
        out_shape=out_shape,
        scratch_shapes=[pltpu.VMEM((N_DSCR, MXU_N // LANES, tm, LANES), F32)],
        compiler_params=pltpu.CompilerParams(dimension_semantics=("arbitrary",),
                                             vmem_limit_bytes=VMEM_LIMIT),
        name="prep",
    )(x2, *consts, *tabs, *consts2)


def _mla_kernel(q_ref, k_ref, v_ref, o_ref, acc_ref, m_ref):
    t = MLA_T
    nt = SEQ // t
    hq = t // 2
    lane = lax.broadcasted_iota(jnp.int32, (1, LANES), 1)
    lo = lane < HALF
    top_mask = (lax.broadcasted_iota(jnp.int32, (hq, hq), 1)
                <= lax.broadcasted_iota(jnp.int32, (hq, hq), 0))
    bot_mask = (lax.broadcasted_iota(jnp.int32, (hq, t), 1)
                <= lax.broadcasted_iota(jnp.int32, (hq, t), 0) + hq)

    def update(hd, r0, nr, k0, nk, mask, first, last):
        sl = slice(HEAD_PAD * hd, HEAD_PAD * (hd + 1))
        rows = slice(r0, r0 + nr)
        s = _dot_t(q_ref[rows, sl], k_ref[k0:k0 + nk, sl])
        if mask is not None:
            s = jnp.where(mask, s, NEG)
        m_new = jnp.broadcast_to(jnp.max(s, axis=-1, keepdims=True), (nr, LANES))
        if not first:
            m_old = m_ref[hd, rows, :]
            m_new = jnp.maximum(m_old, m_new)
        p = jnp.exp(s - jnp.concatenate([m_new] * (nk // LANES), axis=1)).astype(BF16)
        acc = _dot(p, v_ref[k0:k0 + nk, sl])
        if not first:
            acc = acc_ref[hd, rows, :] * jnp.exp(m_old - m_new) + acc
        if last:
            return acc / pltpu.roll(acc, HALF, 1)
        acc_ref[hd, rows, :] = acc
        m_ref[hd, rows, :] = m_new
        return None

    for j in range(nt):
        for i in range(j, nt):
            if i > j:
                for hd in range(2):
                    update(hd, i * t, t, j * t, t, None, j == 0, False)
            else:
                for r0, nr, nk, mask in ((i * t, hq, hq, top_mask), (i * t + hq, hq, t, bot_mask)):
                    o0, o1 = [update(hd, r0, nr, j * t, nk, mask, j == 0, True) for hd in range(2)]
                    o_ref[r0:r0 + nr, :] = jnp.where(lo, o0, o1).astype(BF16)


def _mla_call(q, k, v, B):
    T = q.shape[0]
    blk = pl.BlockSpec((SEQ, 2 * HEAD_PAD), lambda b, hp: (b, hp))
    return pl.pallas_call(
        _mla_kernel,
        grid=(B, MLA_HEADS // 2),
        in_specs=[blk, blk, blk],
        out_specs=pl.BlockSpec((SEQ, LANES), lambda b, hp: (b, hp)),
        out_shape=jax.ShapeDtypeStruct((T, MLA_HEADS * MLA_V), BF16),
        scratch_shapes=[pltpu.VMEM((2, SEQ, LANES), F32),
                        pltpu.VMEM((2, SEQ, LANES), F32)],
        compiler_params=pltpu.CompilerParams(dimension_semantics=("arbitrary", "arbitrary"),
                                             vmem_limit_bytes=VMEM_LIMIT),
        name="mla",
    )(q, k, v)


def _dil_kernel(bias_ref, q0_ref, k0_ref, v0_ref, q1_ref, k1_ref, v1_ref, q2_ref, k2_ref, v2_ref,
                o_ref, n_ref, l_ref, m_ref):
    lane = lax.broadcasted_iota(jnp.int32, (1, LANES), 1)
    lo = lane < HALF
    Q = Q_BLOCK
    ins = ((q0_ref, k0_ref, v0_ref), (q1_ref, k1_ref, v1_ref), (q2_ref, k2_ref, v2_ref))
    ones = jnp.ones((2 * Q, LANES), BF16)

    for g, (_, d) in enumerate(DIL_PATTERNS):
        nblk = SEQ // d // Q
        qg, kg, vg = ins[g]
        bias = bias_ref[g]
        for r in range(d):
            for n in range(nblk):
                def rows_of(ref, a, b):
                    return ref[a:b, :] if d == 1 else ref[r, a:b, :]
                k0 = max(n - 1, 0) * Q
                nk = (n + 1) * Q - k0
                q = rows_of(qg, n * Q, (n + 1) * Q)
                zero = jnp.zeros_like(q)
                qb = jnp.concatenate([jnp.where(lo, q, zero), jnp.where(lo, zero, q)], axis=0)
                s = _dot_t(qb, rows_of(kg, k0, k0 + nk)) + bias[:, 2 * Q - nk:]
                m = jnp.max(s, axis=-1, keepdims=True)
                p = jnp.exp(s - m).astype(BF16)
                vb = jnp.concatenate([rows_of(vg, k0, k0 + nk), ones[:nk]], axis=1)
                acc = _dot(p, vb)
                mb = jnp.broadcast_to(m, (2 * Q, LANES))
                if d == 1:
                    rows = slice(n * Q, (n + 1) * Q)
                else:
                    rows = pl.ds(n * Q * d + r, Q, stride=d)
                n_ref[g, rows, :] = jnp.where(lo, acc[:Q, :LANES], acc[Q:, :LANES])
                l_ref[g, rows, :] = jnp.where(lo, acc[:Q, LANES:], acc[Q:, LANES:])
                m_ref[g, rows, :] = jnp.where(lo, mb[:Q], mb[Q:])

    chunk = 256

    def merge(c, _):
        rows = pl.ds(pl.multiple_of(c * chunk, chunk), chunk)
        ms = [m_ref[g, rows, :] for g in range(DIL_GROUPS)]
        mx = jnp.maximum(jnp.maximum(ms[0], ms[1]), ms[2])
        num = jnp.zeros((chunk, LANES), F32)
        den = jnp.zeros((chunk, LANES), F32)
        for g in range(DIL_GROUPS):
            w = jnp.exp(ms[g] - mx)
            num = num + w * n_ref[g, rows, :]
            den = den + w * l_ref[g, rows, :]
        o_ref[rows, :] = (num / den).astype(BF16)
        return 0

    lax.fori_loop(0, SEQ // chunk, merge, 0)


def _dil_call(bias, dil_in, B):
    T = B * SEQ
    nhp = DIL_HEADS // 2
    in_specs = [pl.BlockSpec((DIL_GROUPS, None, 2 * Q_BLOCK, 2 * Q_BLOCK), lambda b, hp: (0, hp, 0, 0))]
    for _, d in DIL_PATTERNS:
        if d == 1:
            spec = pl.BlockSpec((SEQ, LANES), lambda b, hp: (b, hp))
        else:
            spec = pl.BlockSpec((None, d, SEQ // d, LANES), lambda b, hp: (b, 0, 0, hp))
        in_specs += [spec] * 3
    return pl.pallas_call(
        _dil_kernel,
        grid=(B, nhp),
        in_specs=in_specs,
        out_specs=pl.BlockSpec((SEQ, LANES), lambda b, hp: (b, hp)),
        out_shape=jax.ShapeDtypeStruct((T, DIL_WIDTH), BF16),
        scratch_shapes=[pltpu.VMEM((DIL_GROUPS, SEQ, LANES), F32),
                        pltpu.VMEM((DIL_GROUPS, SEQ, LANES), F32),
                        pltpu.VMEM((DIL_GROUPS, SEQ, LANES), F32)],
        compiler_params=pltpu.CompilerParams(dimension_semantics=("arbitrary", "arbitrary"),
                                             vmem_limit_bytes=VMEM_LIMIT),
        name="dil",
    )(bias, *dil_in)


def _out_kernel(x_ref, ob_ref, oc_ref, ng_ref, wa_ref, wz_ref, wg_ref, bg_ref, cw_ref, cb_ref,
                woa_ref, wob_ref, woc_ref, wo_ref, out_ref, ubuf):
    tm = ROW_TILE
    i = pl.program_id(0)
    x = x_ref[...]
    h = (x * _rms_scale(x, D_MODEL) * ng_ref[...]).astype(BF16)

    @pl.when(i % (SEQ // tm) == 0)
    def _():
        ubuf[0:8, :] = jnp.zeros((8, CONV_WIDTH), F32)

    W = CONV_WIDTH
    u = _dot(h, wa_ref[:, W:2 * W]) * _dot(h, wa_ref[:, 2 * W:3 * W])
    ubuf[8:8 + tm, :] = u
    conv = (cb_ref[...] + ubuf[6:6 + tm, :] * cw_ref[0:1, :] + ubuf[7:7 + tm, :] * cw_ref[1:2, :]
            + u * cw_ref[2:3, :])
    ubuf[0:8, :] = u[tm - 8:tm, :]
    az = _dot(h, wa_ref[:, 3 * W:4 * W])
    ya = _dot(h, wa_ref[:, 0:W]) * conv * (az * _sigmoid(az))

    D = D_MODEL
    ga = _sigmoid(_dot(h, wg_ref[:, 0:D]) + bg_ref[:, 0:D])
    merged = ga * _dot(ya.astype(BF16), woa_ref[...])

    zb = _dot(h, wz_ref[:, 0:W])
    yb = ob_ref[...].astype(F32) * (zb * _sigmoid(zb))
    gb = _sigmoid(_dot(h, wg_ref[:, D:2 * D]) + bg_ref[:, D:2 * D])
    merged = merged + gb * _dot(yb.astype(BF16), wob_ref[...])

    zc = _dot(h, wz_ref[:, W:2 * W])
    yc = oc_ref[...].astype(F32) * (zc * _sigmoid(zc))
    gc = _sigmoid(_dot(h, wg_ref[:, 2 * D:3 * D]) + bg_ref[:, 2 * D:3 * D])
    merged = merged + gc * _dot(yc.astype(BF16), woc_ref[...])

    out_ref[...] = x + _dot(merged.astype(BF16), wo_ref[...])


def _out_call(x2, ob, oc, ow):
    T = x2.shape[0]
    tm = ROW_TILE
    row = lambda n: pl.BlockSpec((tm, n), lambda i: (i, 0))
    consts = [ow['ng'], ow['wa'], ow['wz'], ow['wg'], ow['bg'], ow['cw'], ow['cb'],
              ow['woa'], ow['wob'], ow['woc'], ow['wo']]
    return pl.pallas_call(
        _out_kernel,
        grid=(T // tm,),
        in_specs=[row(D_MODEL), row(MLA_HEADS * MLA_V), row(DIL_WIDTH)]
                 + [_const_spec(a.shape) for a in consts],
        out_specs=row(D_MODEL),
        out_shape=jax.ShapeDtypeStruct((T, D_MODEL), F32),
        scratch_shapes=[pltpu.VMEM((tm + 8, CONV_WIDTH), F32)],
        compiler_params=pltpu.CompilerParams(dimension_semantics=("arbitrary",),
                                             vmem_limit_bytes=VMEM_LIMIT),
        name="out",
    )(x2, ob, oc, *consts)


def _rope_tables():
    inv = ROPE_THETA ** (-jnp.arange(0, MLA_ROPE, 2, dtype=F32) / MLA_ROPE)
    ang = jnp.arange(SEQ, dtype=F32)[:, None] * inv[None, :]
    return jnp.cos(ang), jnp.sin(ang)


def _head_tables(g, cos, sin, scale):
    hr = MLA_ROPE // 2
    g1, g2 = g[MLA_NOPE:MLA_NOPE + hr], g[MLA_NOPE + hr:MLA_QK]
    ones = jnp.ones((SEQ, 1), F32)
    zpad = jnp.zeros((SEQ, HEAD_PAD - MLA_QK), F32)
    gc = jnp.concatenate([ones * g[None, :MLA_NOPE], cos * g1[None], cos * g2[None], zpad], axis=1)
    gs = jnp.concatenate([jnp.zeros((SEQ, MLA_NOPE), F32), sin * g2[None], sin * g1[None], zpad], axis=1)
    return gc * scale, gs * scale


def _rot_half_cols(w):
    hr = MLA_ROPE // 2
    return jnp.concatenate([-w[..., hr:], w[..., :hr]], axis=-1)


def _dil_bias():
    n = DIL_GROUPS * DIL_HEADS
    slopes = (2.0 ** (-8.0 * jnp.arange(1, n + 1, dtype=F32) / n)).reshape(DIL_GROUPS, DIL_HEADS)
    qq = jnp.arange(Q_BLOCK)[:, None]
    kk = jnp.arange(2 * Q_BLOCK)[None, :]
    j = Q_BLOCK + qq - kk
    tabs = []
    for gi, (window, d) in enumerate(DIL_PATTERNS):
        valid = (j >= 0) & (j <= window // d)
        dist = (d * j).astype(F32)
        tabs.append(jnp.where(valid[None], -slopes[gi][:, None, None] * dist[None], NEG))
    tab = jnp.stack(tabs, axis=0)
    return tab.reshape(DIL_GROUPS, DIL_HEADS // 2, 2 * Q_BLOCK, 2 * Q_BLOCK)


def _layer_params(l, p, cos, sin):
    w = p['w_in'][l]
    D = D_MODEL

    def cols(a, b):
        return w[:, a:b]

    zc = lambda n: jnp.zeros((D, n), F32)
    kpe_w = cols(_O_KPE, _O_BZ)
    wlat = jnp.concatenate([cols(_O_CQ, _O_CKV), cols(_O_CKV, _O_KPE),
                            zc(MLA_NOPE), kpe_w, zc(HEAD_PAD - MLA_QK),
                            zc(MLA_NOPE), _rot_half_cols(kpe_w), zc(HEAD_PAD - MLA_QK)], axis=1)

    wuq = p['w_uq'][l].reshape(MLA_Q_LORA, MLA_HEADS, MLA_QK)
    zq = jnp.zeros((MLA_Q_LORA, MLA_HEADS, HEAD_PAD - MLA_QK), F32)
    wuq_p = jnp.concatenate([wuq, zq], axis=-1).reshape(MLA_Q_LORA, MLA_HEADS * HEAD_PAD)
    wuq_r = jnp.concatenate([jnp.zeros((MLA_Q_LORA, MLA_HEADS, MLA_NOPE), F32),
                             _rot_half_cols(wuq[..., MLA_NOPE:]), zq], axis=-1)
    wuq_r = wuq_r.reshape(MLA_Q_LORA, MLA_HEADS * HEAD_PAD)

    wukv = p['w_ukv'][l].reshape(MLA_KV_LORA, MLA_HEADS, MLA_NOPE + MLA_V)
    zk = jnp.zeros((MLA_KV_LORA, MLA_HEADS, HALF), F32)
    wk = jnp.concatenate([wukv[..., :MLA_NOPE], zk], axis=-1).reshape(MLA_KV_LORA, MLA_HEADS * HEAD_PAD)
    wv4 = wukv[..., MLA_NOPE:].reshape(MLA_KV_LORA, MLA_HEADS // 2, 2, MLA_V)
    zv = jnp.zeros((MLA_KV_LORA, MLA_HEADS // 2, MLA_V), F32)
    wv = jnp.stack([jnp.concatenate([wv4[:, :, 0], zv], axis=-1),
                    jnp.concatenate([zv, wv4[:, :, 1]], axis=-1)], axis=2)
    wv = wv.reshape(MLA_KV_LORA, MLA_HEADS * HEAD_PAD)
    half_pat = jnp.concatenate([jnp.zeros((HALF,), F32), jnp.ones((HALF,), F32)])
    vone = jnp.concatenate([half_pat, 1.0 - half_pat] * (MLA_HEADS // 2))[None, :]

    gcq, gsq = _head_tables(p['mla_q_norm_g'][l], cos, sin, MLA_QK ** -0.5)
    gck, gsk = _head_tables(p['mla_k_norm_g'][l], cos, sin, 1.0)

    wd = jnp.concatenate([cols(_O_DQ, _O_DK), cols(_O_DK, _O_DV), cols(_O_DV, _O_CZ)], axis=1)
    gdq = (jnp.tile(p['dil_q_norm_g'][l][:, None, :], (1, DIL_HEADS, 1)).reshape(1, -1)
           * DIL_HEAD_DIM ** -0.5)
    gdk = jnp.tile(p['dil_k_norm_g'][l][:, None, :], (1, DIL_HEADS, 1)).reshape(1, -1)
    ii = jnp.arange(MXU_N) // DIL_HEAD_DIM
    bd = (ii[:, None] == ii[None, :]).astype(BF16)

    ng = p['norm_g'][l][None, :]
    prep = dict(ng=ng, wlat=wlat.astype(BF16), gqa=p['q_a_norm_g'][l][None, :],
                gkva=p['kv_a_norm_g'][l][None, :], wuq=wuq_p.astype(BF16), wuqr=wuq_r.astype(BF16),
                wk=wk.astype(BF16), wv=wv.astype(BF16), vone=vone, gcq=gcq, gsq=gsq, gck=gck, gsk=gsk,
                wd=wd.astype(BF16), gdq=gdq, gdk=gdk, bd=bd)
    out = dict(ng=ng, wa=cols(_O_AB, _O_CQ).astype(BF16),
               wz=jnp.concatenate([cols(_O_BZ, _O_DQ), cols(_O_CZ, _O_GATE)], axis=1).astype(BF16),
               wg=cols(_O_GATE, _O_END).astype(BF16), bg=p['b_gate'][l][None, :],
               cw=p['conv_w'][l], cb=p['conv_b'][l][None, :],
               woa=p['w_out_a'][l].astype(BF16), wob=p['w_out_b'][l].astype(BF16),
               woc=p['w_out_c'][l].astype(BF16), wo=p['w_o'][l].astype(BF16))
    return prep, out


def kernel(x, norm_g, w_in, b_gate, conv_w, conv_b, q_a_norm_g, w_uq, kv_a_norm_g, w_ukv, mla_q_norm_g,
           mla_k_norm_g, dil_q_norm_g, dil_k_norm_g, w_out_a, w_out_b, w_out_c, w_o):
    B, S, D = x.shape
    assert S == SEQ and D == D_MODEL
    T = B * S
    p = dict(norm_g=norm_g, w_in=w_in, b_gate=b_gate, conv_w=conv_w, conv_b=conv_b,
             q_a_norm_g=q_a_norm_g, w_uq=w_uq, kv_a_norm_g=kv_a_norm_g, w_ukv=w_ukv,
             mla_q_norm_g=mla_q_norm_g, mla_k_norm_g=mla_k_norm_g, dil_q_norm_g=dil_q_norm_g,
             dil_k_norm_g=dil_k_norm_g, w_out_a=w_out_a, w_out_b=w_out_b, w_out_c=w_out_c, w_o=w_o)
    cos, sin = _rope_tables()
    bias = _dil_bias()
    x2 = x.reshape(T, D)
    for l in range(DEPTH):
        pw, ow = _layer_params(l, p, cos, sin)
        q, k, v, *dil_in = _prep_call(x2, pw, B)
        ob = _mla_call(q, k, v, B)
        oc = _dil_call(bias, dil_in, B)
        x2 = _out_call(x2, ob, oc, ow)
    return x2.reshape(B, S, D)
```

```python
import numpy as np
import jax
import jax.numpy as jnp
from jax import lax
from jax.experimental import pallas as pl
from jax.experimental.pallas import tpu as pltpu

D_MODEL = 1024
SEQ = 2048
DEPTH = 2
CONV_WIDTH = 512
CONV_K = 3
MLA_HEADS = 8
MLA_Q_LORA = 256
MLA_KV_LORA = 128
MLA_NOPE = 64
MLA_ROPE = 32
MLA_V = 64
MLA_QK = MLA_NOPE + MLA_ROPE
ROPE_THETA = 10000.0
DIL_PATTERNS = ((128, 1), (512, 4), (2048, 16))
DIL_GROUPS = len(DIL_PATTERNS)
DIL_HEADS = 8
DIL_HEAD_DIM = 64
DIL_WIDTH = DIL_HEADS * DIL_HEAD_DIM
N_BRANCH = 3
Q_BLOCK = 128
EPS = 1e-6

SPLIT_SIZES = ((CONV_WIDTH,) * 4
               + (MLA_Q_LORA, MLA_KV_LORA, MLA_ROPE, MLA_HEADS * MLA_V)
               + (DIL_GROUPS * DIL_WIDTH,) * 3 + (DIL_WIDTH,)
               + (N_BRANCH * D_MODEL,))
_OFFS = tuple(int(v) for v in np.cumsum((0,) + SPLIT_SIZES))
(_O_AB, _O_AC, _O_AX, _O_AZ, _O_CQ, _O_CKV, _O_KPE, _O_BZ, _O_DQ, _O_DK, _O_DV, _O_CZ, _O_GATE,
 _O_END) = _OFFS

LANES = 128
HEAD_PAD = LANES
HALF = LANES // 2
MXU_N = 256
NEG = -1e30

BF16 = jnp.bfloat16
F32 = jnp.float32

ROW_TILE = 512
MLA_T = 512
N_DSCR = 4
MERGE_D = 4
MLA_LAG = 1
DIL_LAG = 3
LOG2E = 1.4426950408889634
VMEM_LIMIT = 58 * 1024 * 1024


def _rms_scale(v, n):
    return lax.rsqrt(jnp.sum(v * v, axis=-1, keepdims=True) * (1.0 / n) + EPS)


def _sigmoid(v):
    return 1.0 / (1.0 + jnp.exp(-v))


def _dot(a, b):
    return jnp.dot(a, b, preferred_element_type=F32)


def _dot_t(a, b):
    return lax.dot_general(a, b, (((1,), (1,)), ((), ())), preferred_element_type=F32)


def _const_spec(shape):
    nd = len(shape)
    return pl.BlockSpec(shape, lambda *_: (0,) * nd, pipeline_mode=pl.Buffered(1))


def _prep_kernel(x_ref, ng_ref, wlat_ref, gqa_ref, gkva_ref, wuq_ref, wuqr_ref, wk_ref, wv_ref,
                 vone_ref, gcq_ref, gsq_ref, gck_ref, gsk_ref, wd_ref, gdq_ref, gdk_ref, bd_ref,
                 q_ref, k_ref, v_ref,
                 dq0_ref, dk0_ref, dv0_ref, dq1_ref, dk1_ref, dv1_ref, dq2_ref, dk2_ref, dv2_ref,
                 dscr):
    tm = ROW_TILE
    x = x_ref[...]
    h = (x * _rms_scale(x, D_MODEL) * ng_ref[...]).astype(BF16)

    lat = _dot(h, wlat_ref[...])
    cq = lat[:, :MLA_Q_LORA]
    ckv = lat[:, MLA_Q_LORA:MLA_Q_LORA + MLA_KV_LORA]
    kpe = lat[:, 384:512]
    kper = lat[:, 512:640]
    cqn = (cq * _rms_scale(cq, MLA_Q_LORA) * gqa_ref[...]).astype(BF16)
    ckvn = (ckv * _rms_scale(ckv, MLA_KV_LORA) * gkva_ref[...]).astype(BF16)

    qraw = _dot(cqn, wuq_ref[...])
    qrot = _dot(cqn, wuqr_ref[...])
    kn = _dot(ckvn, wk_ref[...])
    vp = _dot(ckvn, wv_ref[...]) + vone_ref[...]
    v_ref[...] = vp.astype(BF16)

    gcq = gcq_ref[...]
    gsq = gsq_ref[...]
    gck = gck_ref[...]
    krot = kper * gsk_ref[...]
    for hd in range(MLA_HEADS):
        sl = slice(HEAD_PAD * hd, HEAD_PAD * (hd + 1))
        qh = qraw[:, sl]
        q_ref[:, sl] = (_rms_scale(qh, MLA_QK) * (qh * gcq + qrot[:, sl] * gsq)).astype(BF16)
        kh = kn[:, sl] + kpe
        k_ref[:, sl] = (_rms_scale(kh, MLA_QK) * (kh * gck + krot)).astype(BF16)

    nd = DIL_GROUPS * DIL_WIDTH
    bd = bd_ref[...]
    outs = ((dq0_ref, dk0_ref, dv0_ref), (dq1_ref, dk1_ref, dv1_ref), (dq2_ref, dk2_ref, dv2_ref))
    n_store = [0]

    def store(kind, c, val):
        g, half = divmod(c, DIL_WIDTH // MXU_N)
        sl = slice(MXU_N * half, MXU_N * (half + 1))
        d = DIL_PATTERNS[g][1]
        ref = outs[g][kind]
        if d == 1:
            ref[:, sl] = val.astype(BF16)
        else:
            slot = n_store[0] % N_DSCR
            n_store[0] += 1
            for cb in range(MXU_N // LANES):
                dscr[slot, cb] = val[:, LANES * cb:LANES * (cb + 1)]
            for r in range(d):
                for cb in range(MXU_N // LANES):
                    c0 = MXU_N * half + LANES * cb
                    ref[r, :, c0:c0 + LANES] = dscr[slot, cb, pl.ds(r, tm // d, stride=d), :].astype(BF16)

    def finish(kind, c, t):
        g_ref = gdq_ref if kind == 0 else gdk_ref
        sl = slice(MXU_N * c, MXU_N * (c + 1))
        ssq = _dot((t * t).astype(BF16), bd)
        store(kind, c, t * lax.rsqrt(ssq * (1.0 / DIL_HEAD_DIM) + EPS) * g_ref[:, sl])

    pending = None
    nch = nd // MXU_N
    for kind in range(2):
        for c in range(nch):
            col = kind * nd + MXU_N * c
            t = _dot(h, wd_ref[:, col:col + MXU_N])
            if pending is not None:
                finish(*pending)
            pending = (kind, c, t)
            if kind == 1:
                store(2, c, _dot(h, wd_ref[:, 2 * nd + MXU_N * c: 2 * nd + MXU_N * (c + 1)]))
    finish(*pending)


def _prep_call(x2, pw, B):
    T = x2.shape[0]
    tm = ROW_TILE
    nst = SEQ // tm
    row = lambda n: pl.BlockSpec((tm, n), lambda i: (i, 0))
    tab = pl.BlockSpec((tm, LANES), lambda i: (i % nst, 0))
    consts = [pw['ng'], pw['wlat'], pw['gqa'], pw['gkva'], pw['wuq'], pw['wuqr'], pw['wk'], pw['wv'],
              pw['vone']]
    tabs = [pw['gcq'], pw['gsq'], pw['gck'], pw['gsk']]
    consts2 = [pw['wd'], pw['gdq'], pw['gdk'], pw['bd']]
    in_specs = ([row(D_MODEL)] + [_const_spec(a.shape) for a in consts] + [tab] * 4
                + [_const_spec(a.shape) for a in consts2])
    out_shape = [jax.ShapeDtypeStruct((T, MLA_HEADS * HEAD_PAD), BF16)] * 3
    out_specs = [row(MLA_HEADS * HEAD_PAD)] * 3
    for _, d in DIL_PATTERNS:
        if d == 1:
            out_shape += [jax.ShapeDtypeStruct((T, DIL_WIDTH), BF16)] * 3
            out_specs += [row(DIL_WIDTH)] * 3
        else:
            out_shape += [jax.ShapeDtypeStruct((B, d, SEQ // d, DIL_WIDTH), BF16)] * 3
            out_specs += [pl.BlockSpec((None, d, tm // d, DIL_WIDTH),
                                       lambda i: (i // nst, 0, i % nst, 0))] * 3
    return pl.pallas_call(
        _prep_kernel,
        grid=(T // tm,),
        in_specs=in_specs,
        out_specs=out_specs,
        out_shape=out_shape,
        scratch_shapes=[pltpu.VMEM((N_DSCR, MXU_N // LANES, tm, LANES), F32)],
        compiler_params=pltpu.CompilerParams(dimension_semantics=("arbitrary",),
                                             vmem_limit_bytes=VMEM_LIMIT),
        name="prep",
    )(x2, *consts, *tabs, *consts2)


def _mla_kernel(q_ref, k_ref, v_ref, o_ref, acc_ref, m_ref):
    t = MLA_T
    nt = SEQ // t
    hq = t // 2
    lane = lax.broadcasted_iota(jnp.int32, (1, LANES), 1)
    lo = lane < HALF
    top_mask = (lax.broadcasted_iota(jnp.int32, (hq, hq), 1)
                <= lax.broadcasted_iota(jnp.int32, (hq, hq), 0))
    bot_mask = (lax.broadcasted_iota(jnp.int32, (hq, t), 1)
                <= lax.broadcasted_iota(jnp.int32, (hq, t), 0) + hq)

    updates = []
    for j in range(nt):
        if j + 1 < nt:
            for hd in range(2):
                updates.append((hd, (j + 1) * t, (nt - 1 - j) * t, j * t, t, None, j == 0, False))
        for r0, nr, nk, mask in ((j * t, hq, hq, top_mask), (j * t + hq, hq, t, bot_mask)):
            for hd in range(2):
                updates.append((hd, r0, nr, j * t, nk, mask, j == 0, True))

    def scores(hd, r0, nr, k0, nk, mask, first, last):
        sl = slice(HEAD_PAD * hd, HEAD_PAD * (hd + 1))
        s = _dot_t(q_ref[r0:r0 + nr, sl], k_ref[k0:k0 + nk, sl])
        return s if mask is None else jnp.where(mask, s, NEG)

    def softmax(s, hd, r0, nr, k0, nk, mask, first, last):
        m_new = jnp.broadcast_to(jnp.max(s, axis=-1, keepdims=True), (nr, LANES))
        m_old = None
        if not first:
            m_old = m_ref[hd, r0:r0 + nr, :]
            m_new = jnp.maximum(m_old, m_new)
        p = jnp.exp2(s - jnp.concatenate([m_new] * (nk // LANES), axis=1)).astype(BF16)
        return m_old, m_new, p

    def weighted_values(p, hd, r0, nr, k0, nk, mask, first, last):
        return _dot(p, v_ref[k0:k0 + nk, HEAD_PAD * hd:HEAD_PAD * (hd + 1)])

    def finish(m_old, m_new, acc, hd, r0, nr, k0, nk, mask, first, last):
        if not first:
            acc = acc_ref[hd, r0:r0 + nr, :] * jnp.exp2(m_old - m_new) + acc
        if last:
            return acc / pltpu.roll(acc, HALF, 1)
        acc_ref[hd, r0:r0 + nr, :] = acc
        m_ref[hd, r0:r0 + nr, :] = m_new
        return None

    nu = len(updates)
    lag = MLA_LAG
    written_at = {}
    ss, mp, pv, done = {}, {}, {}, {}
    for step in range(nu + 3 * lag):
        if step < nu:
            ss[step] = scores(*updates[step])
        u = step - lag
        if 0 <= u < nu:
            hd, r0, nr = updates[u][:3]
            for rr in range(r0, r0 + nr, hq):
                assert updates[u][6] or written_at[(hd, rr)] < step, "softmax state read too early"
            mp[u] = softmax(ss.pop(u), *updates[u])
        u = step - 2 * lag
        if 0 <= u < nu:
            pv[u] = weighted_values(mp[u][2], *updates[u])
        u = step - 3 * lag
        if 0 <= u < nu:
            hd, r0, nr = updates[u][:3]
            m_old, m_new, _ = mp.pop(u)
            done[u] = finish(m_old, m_new, pv.pop(u), *updates[u])
            for rr in range(r0, r0 + nr, hq):
                written_at[(hd, rr)] = step
            if updates[u][7] and hd == 1:
                o_ref[r0:r0 + nr, :] = jnp.where(lo, done.pop(u - 1), done.pop(u)).astype(BF16)


def _mla_call(q, k, v, B):
    T = q.shape[0]
    blk = pl.BlockSpec((SEQ, 2 * HEAD_PAD), lambda b, hp: (b, hp))
    return pl.pallas_call(
        _mla_kernel,
        grid=(B, MLA_HEADS // 2),
        in_specs=[blk, blk, blk],
        out_specs=pl.BlockSpec((SEQ, LANES), lambda b, hp: (b, hp)),
        out_shape=jax.ShapeDtypeStruct((T, MLA_HEADS * MLA_V), BF16),
        scratch_shapes=[pltpu.VMEM((2, SEQ, LANES), F32),
                        pltpu.VMEM((2, SEQ, LANES), F32)],
        compiler_params=pltpu.CompilerParams(dimension_semantics=("arbitrary", "arbitrary"),
                                             vmem_limit_bytes=VMEM_LIMIT),
        name="mla",
    )(q, k, v)


def _dil_kernel(bias_ref, q0_ref, k0_ref, v0_ref, q1_ref, k1_ref, v1_ref, q2_ref, k2_ref, v2_ref,
                o_ref, og_ref, lse_ref, nat_ref):
    lane = lax.broadcasted_iota(jnp.int32, (1, LANES), 1)
    lo = lane < HALF
    Q = Q_BLOCK
    ins = ((q0_ref, k0_ref, v0_ref), (q1_ref, k1_ref, v1_ref), (q2_ref, k2_ref, v2_ref))
    ones = jnp.ones((2 * Q, LANES), BF16)

    def rows_of(ref, g, r, a, b):
        return ref[a:b, :] if DIL_PATTERNS[g][1] == 1 else ref[r, a:b, :]

    def key_window(n):
        k0 = max(n - 1, 0) * Q
        return k0, (n + 1) * Q - k0

    def scores(g, r, n):
        k0, nk = key_window(n)
        q = rows_of(ins[g][0], g, r, n * Q, (n + 1) * Q)
        zero = jnp.zeros_like(q)
        qb = jnp.concatenate([jnp.where(lo, q, zero), jnp.where(lo, zero, q)], axis=0)
        return _dot_t(qb, rows_of(ins[g][1], g, r, k0, k0 + nk)) + bias_ref[g, :, 2 * Q - nk:]

    def softmax(s):
        m = jnp.max(s, axis=-1, keepdims=True)
        return m, jnp.exp2(s - m).astype(BF16)

    def weighted_values(g, r, n, p):
        k0, nk = key_window(n)
        vb = jnp.concatenate([rows_of(ins[g][2], g, r, k0, k0 + nk), ones[:nk]], axis=1)
        return _dot(p, vb)

    def finish(g, r, n, m, acc):
        d = DIL_PATTERNS[g][1]
        mb = jnp.broadcast_to(m, (2 * Q, LANES))
        den = jnp.where(lo, acc[:Q, LANES:], acc[Q:, LANES:])
        if d == 1:
            rows = slice(n * Q, (n + 1) * Q)
        elif d == MERGE_D:
            rows = slice(r * (SEQ // d) + n * Q, r * (SEQ // d) + (n + 1) * Q)
        else:
            rows = pl.ds((r % MERGE_D) * (SEQ // MERGE_D) + r // MERGE_D, Q, stride=MERGE_D)
        og_ref[g, rows, :] = jnp.where(lo, acc[:Q, :LANES], acc[Q:, :LANES]) / den
        lse_ref[g, rows, :] = jnp.where(lo, mb[:Q], mb[Q:]) + jnp.log2(den)

    blocks = [(g, r, n) for g, (_, d) in enumerate(DIL_PATTERNS)
              for r in range(d) for n in range(SEQ // d // Q)]
    nb = len(blocks)
    ss, mp, accs = {}, {}, {}
    for step in range(nb + 3 * DIL_LAG):
        if step < nb:
            ss[step] = scores(*blocks[step])
        b = step - DIL_LAG
        if 0 <= b < nb:
            mp[b] = softmax(ss.pop(b))
        b = step - 2 * DIL_LAG
        if 0 <= b < nb:
            accs[b] = weighted_values(*blocks[b], mp[b][1])
        b = step - 3 * DIL_LAG
        if 0 <= b < nb:
            finish(*blocks[b], mp.pop(b)[0], accs.pop(b))


    chunk = 256
    per_class = SEQ // MERGE_D
    for c in range(SEQ // chunk):
        r0, i0 = divmod(c * chunk, per_class)
        cm = slice(c * chunk, (c + 1) * chunk)
        nat = pl.ds(MERGE_D * i0 + r0, chunk, stride=MERGE_D)
        rows = (nat, cm, cm)
        ls = [lse_ref[g, rows[g], :] for g in range(DIL_GROUPS)]
        mx = jnp.maximum(jnp.maximum(ls[0], ls[1]), ls[2])
        num = jnp.zeros((chunk, LANES), F32)
        den = jnp.zeros((chunk, LANES), F32)
        for g in range(DIL_GROUPS):
            w = jnp.exp2(ls[g] - mx)
            num = num + w * og_ref[g, rows[g], :]
            den = den + w
        nat_ref[nat, :] = num / den
    o_ref[...] = nat_ref[...].astype(BF16)


def _dil_call(bias, dil_in, B):
    T = B * SEQ
    nhp = DIL_HEADS // 2
    in_specs = [pl.BlockSpec((DIL_GROUPS, None, 2 * Q_BLOCK, 2 * Q_BLOCK), lambda b, hp: (0, hp, 0, 0))]
    for _, d in DIL_PATTERNS:
        if d == 1:
            spec = pl.BlockSpec((SEQ, LANES), lambda b, hp: (b, hp))
        else:
            spec = pl.BlockSpec((None, d, SEQ // d, LANES), lambda b, hp: (b, 0, 0, hp))
        in_specs += [spec] * 3
    return pl.pallas_call(
        _dil_kernel,
        grid=(B, nhp),
        in_specs=in_specs,
        out_specs=pl.BlockSpec((SEQ, LANES), lambda b, hp: (b, hp)),
        out_shape=jax.ShapeDtypeStruct((T, DIL_WIDTH), BF16),
        scratch_shapes=[pltpu.VMEM((DIL_GROUPS, SEQ, LANES), F32),
                        pltpu.VMEM((DIL_GROUPS, SEQ, LANES), F32),
                        pltpu.VMEM((SEQ, LANES), F32)],
        compiler_params=pltpu.CompilerParams(dimension_semantics=("arbitrary", "arbitrary"),
                                             vmem_limit_bytes=VMEM_LIMIT),
        name="dil",
    )(bias, *dil_in)


def _out_kernel(x_ref, ob_ref, oc_ref, ng_ref, wa_ref, wz_ref, wg_ref, bg_ref, cw_ref, cb_ref,
                woa_ref, wob_ref, woc_ref, wo_ref, out_ref, ubuf):
    tm = ROW_TILE
    i = pl.program_id(0)
    x = x_ref[...]
    h = (x * _rms_scale(x, D_MODEL) * ng_ref[...]).astype(BF16)

    @pl.when(i % (SEQ // tm) == 0)
    def _():
        ubuf[0:8, :] = jnp.zeros((8, CONV_WIDTH), F32)

    W = CONV_WIDTH
    u = _dot(h, wa_ref[:, W:2 * W]) * _dot(h, wa_ref[:, 2 * W:3 * W])
    ubuf[8:8 + tm, :] = u
    conv = (cb_ref[...] + ubuf[6:6 + tm, :] * cw_ref[0:1, :] + ubuf[7:7 + tm, :] * cw_ref[1:2, :]
            + u * cw_ref[2:3, :])
    ubuf[0:8, :] = u[tm - 8:tm, :]
    az = _dot(h, wa_ref[:, 3 * W:4 * W])
    ya = _dot(h, wa_ref[:, 0:W]) * conv * (az * _sigmoid(az))

    D = D_MODEL
    ga = _sigmoid(_dot(h, wg_ref[:, 0:D]) + bg_ref[:, 0:D])
    merged = ga * _dot(ya.astype(BF16), woa_ref[...])

    zb = _dot(h, wz_ref[:, 0:W])
    yb = ob_ref[...].astype(F32) * (zb * _sigmoid(zb))
    gb = _sigmoid(_dot(h, wg_ref[:, D:2 * D]) + bg_ref[:, D:2 * D])
    merged = merged + gb * _dot(yb.astype(BF16), wob_ref[...])

    zc = _dot(h, wz_ref[:, W:2 * W])
    yc = oc_ref[...].astype(F32) * (zc * _sigmoid(zc))
    gc = _sigmoid(_dot(h, wg_ref[:, 2 * D:3 * D]) + bg_ref[:, 2 * D:3 * D])
    merged = merged + gc * _dot(yc.astype(BF16), woc_ref[...])

    out_ref[...] = x + _dot(merged.astype(BF16), wo_ref[...])


def _out_call(x2, ob, oc, ow):
    T = x2.shape[0]
    tm = ROW_TILE
    row = lambda n: pl.BlockSpec((tm, n), lambda i: (i, 0))
    consts = [ow['ng'], ow['wa'], ow['wz'], ow['wg'], ow['bg'], ow['cw'], ow['cb'],
              ow['woa'], ow['wob'], ow['woc'], ow['wo']]
    return pl.pallas_call(
        _out_kernel,
        grid=(T // tm,),
        in_specs=[row(D_MODEL), row(MLA_HEADS * MLA_V), row(DIL_WIDTH)]
                 + [_const_spec(a.shape) for a in consts],
        out_specs=row(D_MODEL),
        out_shape=jax.ShapeDtypeStruct((T, D_MODEL), F32),
        scratch_shapes=[pltpu.VMEM((tm + 8, CONV_WIDTH), F32)],
        compiler_params=pltpu.CompilerParams(dimension_semantics=("arbitrary",),
                                             vmem_limit_bytes=VMEM_LIMIT),
        name="out",
    )(x2, ob, oc, *consts)


def _rope_tables():
    inv = ROPE_THETA ** (-jnp.arange(0, MLA_ROPE, 2, dtype=F32) / MLA_ROPE)
    ang = jnp.arange(SEQ, dtype=F32)[:, None] * inv[None, :]
    return jnp.cos(ang), jnp.sin(ang)


def _head_tables(g, cos, sin, scale):
    hr = MLA_ROPE // 2
    g1, g2 = g[MLA_NOPE:MLA_NOPE + hr], g[MLA_NOPE + hr:MLA_QK]
    ones = jnp.ones((SEQ, 1), F32)
    zpad = jnp.zeros((SEQ, HEAD_PAD - MLA_QK), F32)
    gc = jnp.concatenate([ones * g[None, :MLA_NOPE], cos * g1[None], cos * g2[None], zpad], axis=1)
    gs = jnp.concatenate([jnp.zeros((SEQ, MLA_NOPE), F32), sin * g2[None], sin * g1[None], zpad], axis=1)
    return gc * scale, gs * scale


def _rot_half_cols(w):
    hr = MLA_ROPE // 2
    return jnp.concatenate([-w[..., hr:], w[..., :hr]], axis=-1)


def _dil_bias():
    n = DIL_GROUPS * DIL_HEADS
    slopes = (2.0 ** (-8.0 * jnp.arange(1, n + 1, dtype=F32) / n)).reshape(DIL_GROUPS, DIL_HEADS)
    qq = jnp.arange(Q_BLOCK)[:, None]
    kk = jnp.arange(2 * Q_BLOCK)[None, :]
    j = Q_BLOCK + qq - kk
    tabs = []
    for gi, (window, d) in enumerate(DIL_PATTERNS):
        valid = (j >= 0) & (j <= window // d)
        dist = (d * j).astype(F32)
        tabs.append(jnp.where(valid[None], (-LOG2E) * slopes[gi][:, None, None] * dist[None], NEG))
    tab = jnp.stack(tabs, axis=0)
    return tab.reshape(DIL_GROUPS, DIL_HEADS // 2, 2 * Q_BLOCK, 2 * Q_BLOCK)


def _layer_params(l, p, cos, sin):
    D = D_MODEL

    def cols(a, b):
        return p['w_in'][l, :, a:b]

    zc = lambda n: jnp.zeros((D, n), F32)
    kpe_w = cols(_O_KPE, _O_BZ)
    wlat = jnp.concatenate([cols(_O_CQ, _O_CKV), cols(_O_CKV, _O_KPE),
                            zc(MLA_NOPE), kpe_w, zc(HEAD_PAD - MLA_QK),
                            zc(MLA_NOPE), _rot_half_cols(kpe_w), zc(HEAD_PAD - MLA_QK)], axis=1)

    wuq = p['w_uq'][l].reshape(MLA_Q_LORA, MLA_HEADS, MLA_QK)
    zq = jnp.zeros((MLA_Q_LORA, MLA_HEADS, HEAD_PAD - MLA_QK), F32)
    wuq_p = jnp.concatenate([wuq, zq], axis=-1).reshape(MLA_Q_LORA, MLA_HEADS * HEAD_PAD)
    wuq_r = jnp.concatenate([jnp.zeros((MLA_Q_LORA, MLA_HEADS, MLA_NOPE), F32),
                             _rot_half_cols(wuq[..., MLA_NOPE:]), zq], axis=-1)
    wuq_r = wuq_r.reshape(MLA_Q_LORA, MLA_HEADS * HEAD_PAD)

    wukv = p['w_ukv'][l].reshape(MLA_KV_LORA, MLA_HEADS, MLA_NOPE + MLA_V)
    zk = jnp.zeros((MLA_KV_LORA, MLA_HEADS, HALF), F32)
    wk = jnp.concatenate([wukv[..., :MLA_NOPE], zk], axis=-1).reshape(MLA_KV_LORA, MLA_HEADS * HEAD_PAD)
    wv4 = wukv[..., MLA_NOPE:].reshape(MLA_KV_LORA, MLA_HEADS // 2, 2, MLA_V)
    zv = jnp.zeros((MLA_KV_LORA, MLA_HEADS // 2, MLA_V), F32)
    wv = jnp.stack([jnp.concatenate([wv4[:, :, 0], zv], axis=-1),
                    jnp.concatenate([zv, wv4[:, :, 1]], axis=-1)], axis=2)
    wv = wv.reshape(MLA_KV_LORA, MLA_HEADS * HEAD_PAD)
    half_pat = jnp.concatenate([jnp.zeros((HALF,), F32), jnp.ones((HALF,), F32)])
    vone = jnp.concatenate([half_pat, 1.0 - half_pat] * (MLA_HEADS // 2))[None, :]

    gcq, gsq = _head_tables(p['mla_q_norm_g'][l], cos, sin, MLA_QK ** -0.5 * LOG2E)
    gck, gsk = _head_tables(p['mla_k_norm_g'][l], cos, sin, 1.0)

    wd = jnp.concatenate([cols(_O_DQ, _O_DK), cols(_O_DK, _O_DV), cols(_O_DV, _O_CZ)], axis=1)
    gdq = (jnp.tile(p['dil_q_norm_g'][l][:, None, :], (1, DIL_HEADS, 1)).reshape(1, -1)
           * (DIL_HEAD_DIM ** -0.5 * LOG2E))
    gdk = jnp.tile(p['dil_k_norm_g'][l][:, None, :], (1, DIL_HEADS, 1)).reshape(1, -1)
    ii = jnp.arange(MXU_N) // DIL_HEAD_DIM
    bd = (ii[:, None] == ii[None, :]).astype(BF16)

    ng = p['norm_g'][l][None, :]
    prep = dict(ng=ng, wlat=wlat.astype(BF16), gqa=p['q_a_norm_g'][l][None, :],
                gkva=p['kv_a_norm_g'][l][None, :], wuq=wuq_p.astype(BF16), wuqr=wuq_r.astype(BF16),
                wk=wk.astype(BF16), wv=wv.astype(BF16), vone=vone, gcq=gcq, gsq=gsq, gck=gck, gsk=gsk,
                wd=wd.astype(BF16), gdq=gdq, gdk=gdk, bd=bd)
    out = dict(ng=ng, wa=cols(_O_AB, _O_CQ).astype(BF16),
               wz=jnp.concatenate([cols(_O_BZ, _O_DQ), cols(_O_CZ, _O_GATE)], axis=1).astype(BF16),
               wg=cols(_O_GATE, _O_END).astype(BF16), bg=p['b_gate'][l][None, :],
               cw=p['conv_w'][l], cb=p['conv_b'][l][None, :],
               woa=p['w_out_a'][l].astype(BF16), wob=p['w_out_b'][l].astype(BF16),
               woc=p['w_out_c'][l].astype(BF16), wo=p['w_o'][l].astype(BF16))
    return prep, out


def kernel(x, norm_g, w_in, b_gate, conv_w, conv_b, q_a_norm_g, w_uq, kv_a_norm_g, w_ukv, mla_q_norm_g,
           mla_k_norm_g, dil_q_norm_g, dil_k_norm_g, w_out_a, w_out_b, w_out_c, w_o):
    B, S, D = x.shape
    assert S == SEQ and D == D_MODEL
    T = B * S
    p = dict(norm_g=norm_g, w_in=w_in, b_gate=b_gate, conv_w=conv_w, conv_b=conv_b,
             q_a_norm_g=q_a_norm_g, w_uq=w_uq, kv_a_norm_g=kv_a_norm_g, w_ukv=w_ukv,
             mla_q_norm_g=mla_q_norm_g, mla_k_norm_g=mla_k_norm_g, dil_q_norm_g=dil_q_norm_g,
             dil_k_norm_g=dil_k_norm_g, w_out_a=w_out_a, w_out_b=w_out_b, w_out_c=w_out_c, w_o=w_o)
    cos, sin = _rope_tables()
    bias = _dil_bias()
    x2 = x.reshape(T, D)
    for l in range(DEPTH):
        pw, ow = _layer_params(l, p, cos, sin)
        q, k, v, *dil_in = _prep_call(x2, pw, B)
        ob = _mla_call(q, k, v, B)
        oc = _dil_call(bias, dil_in, B)
        x2 = _out_call(x2, ob, oc, ow)
    return x2.reshape(B, S, D)
```

```python
import numpy as np
import jax
import jax.numpy as jnp
from jax import lax
from jax.experimental import pallas as pl
from jax.experimental.pallas import tpu as pltpu

D_MODEL = 1024
SEQ = 2048
DEPTH = 2
CONV_WIDTH = 512
CONV_K = 3
MLA_HEADS = 8
MLA_Q_LORA = 256
MLA_KV_LORA = 128
MLA_NOPE = 64
MLA_ROPE = 32
MLA_V = 64
MLA_QK = MLA_NOPE + MLA_ROPE
ROPE_THETA = 10000.0
DIL_PATTERNS = ((128, 1), (512, 4), (2048, 16))
DIL_GROUPS = len(DIL_PATTERNS)
DIL_HEADS = 8
DIL_HEAD_DIM = 64
DIL_WIDTH = DIL_HEADS * DIL_HEAD_DIM
N_BRANCH = 3
Q_BLOCK = 128
EPS = 1e-6

SPLIT_SIZES = ((CONV_WIDTH,) * 4
               + (MLA_Q_LORA, MLA_KV_LORA, MLA_ROPE, MLA_HEADS * MLA_V)
               + (DIL_GROUPS * DIL_WIDTH,) * 3 + (DIL_WIDTH,)
               + (N_BRANCH * D_MODEL,))
_OFFS = tuple(int(v) for v in np.cumsum((0,) + SPLIT_SIZES))
(_O_AB, _O_AC, _O_AX, _O_AZ, _O_CQ, _O_CKV, _O_KPE, _O_BZ, _O_DQ, _O_DK, _O_DV, _O_CZ, _O_GATE,
 _O_END) = _OFFS

LANES = 128
HEAD_PAD = LANES
HALF = LANES // 2
MXU_N = 256
NEG = -1e30

BF16 = jnp.bfloat16
F32 = jnp.float32

ROW_TILE = 512
MLA_T = 512
N_DSCR = 4
W_BLOCK = 512
MERGE_D = 4
PREP_INTERLEAVE = (2, 2, 1)
MLA_LAG = 1
DIL_LAG = 3
LOG2E = 1.4426950408889634
VMEM_LIMIT = 58 * 1024 * 1024


def _rms_scale(v, n):
    return lax.rsqrt(jnp.sum(v * v, axis=-1, keepdims=True) * (1.0 / n) + EPS)


def _sigmoid(v):
    return 1.0 / (1.0 + jnp.exp(-v))


def _dot(a, b):
    return jnp.dot(a, b, preferred_element_type=F32)


def _dot_t(a, b):
    return lax.dot_general(a, b, (((1,), (1,)), ((), ())), preferred_element_type=F32)


def _const_spec(shape):
    nd = len(shape)
    return pl.BlockSpec(shape, lambda *_: (0,) * nd, pipeline_mode=pl.Buffered(1))


def _prep_kernel(x_ref, ng_ref, wcq_ref, wckv_ref, wkpe_ref, gqa_ref, gkva_ref, wuq_ref, wk_ref, wv_ref,
                 vone_ref, gcq_ref, gsq_ref, gck_ref, gsk_ref, gdq_ref, gdk_ref, bd_ref, *rest):
    nwd = 3 * DIL_GROUPS * DIL_WIDTH // W_BLOCK
    wd_refs = rest[:nwd]
    (q_ref, k_ref, v_ref, dq0_ref, dk0_ref, dv0_ref, dq1_ref, dk1_ref, dv1_ref,
     dq2_ref, dk2_ref, dv2_ref, dscr) = rest[nwd:]

    def wd_cols(col):
        return wd_refs[col // W_BLOCK][:, col % W_BLOCK:col % W_BLOCK + MXU_N]

    tm = ROW_TILE
    x = x_ref[...]
    h = (x * _rms_scale(x, D_MODEL) * ng_ref[...]).astype(BF16)

    nd = DIL_GROUPS * DIL_WIDTH
    bd = bd_ref[...]
    outs = ((dq0_ref, dk0_ref, dv0_ref), (dq1_ref, dk1_ref, dv1_ref), (dq2_ref, dk2_ref, dv2_ref))
    n_store = [0]

    def store(kind, c, val):
        g, half = divmod(c, DIL_WIDTH // MXU_N)
        sl = slice(MXU_N * half, MXU_N * (half + 1))
        d = DIL_PATTERNS[g][1]
        ref = outs[g][kind]
        if d == 1:
            ref[:, sl] = val.astype(BF16)
        else:
            slot = n_store[0] % N_DSCR
            n_store[0] += 1
            for cb in range(MXU_N // LANES):
                dscr[slot, cb] = val[:, LANES * cb:LANES * (cb + 1)]
            for r in range(d):
                for cb in range(MXU_N // LANES):
                    c0 = MXU_N * half + LANES * cb
                    ref[r, :, c0:c0 + LANES] = dscr[slot, cb, pl.ds(r, tm // d, stride=d), :].astype(BF16)

    def finish(kind, c, t):
        g_ref = gdq_ref if kind == 0 else gdk_ref
        sl = slice(MXU_N * c, MXU_N * (c + 1))
        ssq = _dot((t * t).astype(BF16), bd)
        store(kind, c, t * lax.rsqrt(ssq * (1.0 / DIL_HEAD_DIM) + EPS) * g_ref[:, sl])

    pending = []
    nch = nd // MXU_N

    def dil_chunk(kind, c):
        t = _dot(h, wd_cols(kind * nd + MXU_N * c))
        if pending:
            finish(*pending.pop())
        pending.append((kind, c, t))
        if kind == 1:
            store(2, c, _dot(h, wd_cols(2 * nd + MXU_N * c)))

    chunks = [(kind, c) for kind in range(2) for c in range(nch)]

    def dil_steps(n):
        for _ in range(n):
            if chunks:
                dil_chunk(*chunks.pop(0))


    cq = _dot(h, wcq_ref[...])
    lat = _dot(h, jnp.concatenate([wckv_ref[...], wkpe_ref[...]], axis=1))
    dil_steps(PREP_INTERLEAVE[0])
    ckv = lat[:, :MLA_KV_LORA]
    kpe = lat[:, MLA_KV_LORA:]
    cqn = (cq * _rms_scale(cq, MLA_Q_LORA) * gqa_ref[...]).astype(BF16)
    ckvn = (ckv * _rms_scale(ckv, MLA_KV_LORA) * gkva_ref[...]).astype(BF16)

    qraw = _dot(cqn, wuq_ref[...])
    kn = _dot(ckvn, wk_ref[...])
    vp = _dot(ckvn, wv_ref[...]) + vone_ref[...]
    v_ref[...] = vp.astype(BF16)
    dil_steps(PREP_INTERLEAVE[1])

    lane = lax.broadcasted_iota(jnp.int32, (1, LANES), 1)
    first_half = lane < MLA_NOPE + MLA_ROPE // 2

    def swap_halves(t):
        return jnp.where(first_half, pltpu.roll(t, LANES - MLA_ROPE // 2, 1),
                         pltpu.roll(t, MLA_ROPE // 2, 1))

    gcq = gcq_ref[...]
    gsq = gsq_ref[...]
    gck = gck_ref[...]
    krot = swap_halves(kpe) * gsk_ref[...]
    for hd in range(MLA_HEADS):
        sl = slice(HEAD_PAD * hd, HEAD_PAD * (hd + 1))
        qh = qraw[:, sl]
        q_ref[:, sl] = (_rms_scale(qh, MLA_QK) * (qh * gcq + swap_halves(qh) * gsq)).astype(BF16)
        kh = kn[:, sl] + kpe
        k_ref[:, sl] = (_rms_scale(kh, MLA_QK) * (kh * gck + krot)).astype(BF16)
        dil_steps(PREP_INTERLEAVE[2])

    dil_steps(len(chunks))
    finish(*pending.pop())


def _w_spec(l, width, idx):
    return pl.BlockSpec((None, D_MODEL, width), lambda *_: (l, 0, idx), pipeline_mode=pl.Buffered(1))


def _prep_call(x2, pw, wfirst, wrest, l, B):
    T = x2.shape[0]
    tm = ROW_TILE
    nst = SEQ // tm
    row = lambda n: pl.BlockSpec((tm, n), lambda i: (i, 0))
    tab = pl.BlockSpec((tm, LANES), lambda i: (i % nst, 0))
    consts = [pw['ng']]
    wspecs = [_w_spec(l, MLA_Q_LORA, _O_CQ // MLA_Q_LORA), _w_spec(l, MLA_KV_LORA, _O_CKV // MLA_KV_LORA)]
    consts1 = [pw['wkpe'], pw['gqa'], pw['gkva'], pw['wuq'], pw['wk'], pw['wv'], pw['vone']]
    tabs = [pw['gcq'], pw['gsq'], pw['gck'], pw['gsk']]
    consts2 = [pw['gdq'], pw['gdk'], pw['bd']]
    wd_blocks = range((_O_DQ - _O_BZ) // W_BLOCK, (_O_CZ - _O_BZ) // W_BLOCK)
    in_specs = ([row(D_MODEL)] + [_const_spec(a.shape) for a in consts] + wspecs
                + [_const_spec(a.shape) for a in consts1] + [tab] * 4
                + [_const_spec(a.shape) for a in consts2]
                + [_w_spec(l, W_BLOCK, b) for b in wd_blocks])
    args = ([x2] + consts + [wfirst, wfirst] + consts1 + tabs + consts2 + [wrest] * len(wd_blocks))
    out_shape = [jax.ShapeDtypeStruct((T, MLA_HEADS * HEAD_PAD), BF16)] * 3
    out_specs = [row(MLA_HEADS * HEAD_PAD)] * 3
    for _, d in DIL_PATTERNS:
        if d == 1:
            out_shape += [jax.ShapeDtypeStruct((T, DIL_WIDTH), BF16)] * 3
            out_specs += [row(DIL_WIDTH)] * 3
        else:
            out_shape += [jax.ShapeDtypeStruct((B, d, SEQ // d, DIL_WIDTH), BF16)] * 3
            out_specs += [pl.BlockSpec((None, d, tm // d, DIL_WIDTH),
                                       lambda i: (i // nst, 0, i % nst, 0))] * 3
    return pl.pallas_call(
        _prep_kernel,
        grid=(T // tm,),
        in_specs=in_specs,
        out_specs=out_specs,
        out_shape=out_shape,
        scratch_shapes=[pltpu.VMEM((N_DSCR, MXU_N // LANES, tm, LANES), F32)],
        compiler_params=pltpu.CompilerParams(dimension_semantics=("arbitrary",),
                                             vmem_limit_bytes=VMEM_LIMIT),
        name="prep",
    )(*args)


def _mla_kernel(q_ref, k_ref, v_ref, o_ref, acc_ref, m_ref):
    t = MLA_T
    nt = SEQ // t
    hq = t // 2
    lane = lax.broadcasted_iota(jnp.int32, (1, LANES), 1)
    lo = lane < HALF
    top_mask = (lax.broadcasted_iota(jnp.int32, (hq, hq), 1)
                <= lax.broadcasted_iota(jnp.int32, (hq, hq), 0))
    bot_mask = (lax.broadcasted_iota(jnp.int32, (hq, t), 1)
                <= lax.broadcasted_iota(jnp.int32, (hq, t), 0) + hq)

    updates = []
    for j in range(nt):
        if j + 1 < nt:
            for hd in range(2):
                updates.append((hd, (j + 1) * t, (nt - 1 - j) * t, j * t, t, None, j == 0, False))
        for r0, nr, nk, mask in ((j * t, hq, hq, top_mask), (j * t + hq, hq, t, bot_mask)):
            for hd in range(2):
                updates.append((hd, r0, nr, j * t, nk, mask, j == 0, True))

    def scores(hd, r0, nr, k0, nk, mask, first, last):
        sl = slice(HEAD_PAD * hd, HEAD_PAD * (hd + 1))
        s = _dot_t(q_ref[r0:r0 + nr, sl], k_ref[k0:k0 + nk, sl])
        return s if mask is None else jnp.where(mask, s, NEG)

    def softmax(s, hd, r0, nr, k0, nk, mask, first, last):
        m_new = jnp.broadcast_to(jnp.max(s, axis=-1, keepdims=True), (nr, LANES))
        m_old = None
        if not first:
            m_old = m_ref[hd, r0:r0 + nr, :]
            m_new = jnp.maximum(m_old, m_new)
        p = jnp.exp2(s - jnp.concatenate([m_new] * (nk // LANES), axis=1)).astype(BF16)
        return m_old, m_new, p

    def weighted_values(p, hd, r0, nr, k0, nk, mask, first, last):
        return _dot(p, v_ref[k0:k0 + nk, HEAD_PAD * hd:HEAD_PAD * (hd + 1)])

    def finish(m_old, m_new, acc, hd, r0, nr, k0, nk, mask, first, last):
        if not first:
            acc = acc_ref[hd, r0:r0 + nr, :] * jnp.exp2(m_old - m_new) + acc
        if last:
            return acc / pltpu.roll(acc, HALF, 1)
        acc_ref[hd, r0:r0 + nr, :] = acc
        m_ref[hd, r0:r0 + nr, :] = m_new
        return None

    nu = len(updates)
    lag = MLA_LAG
    written_at = {}
    ss, mp, pv, done = {}, {}, {}, {}
    for step in range(nu + 3 * lag):
        if step < nu:
            ss[step] = scores(*updates[step])
        u = step - lag
        if 0 <= u < nu:
            hd, r0, nr = updates[u][:3]
            for rr in range(r0, r0 + nr, hq):
                assert updates[u][6] or written_at[(hd, rr)] < step, "softmax state read too early"
            mp[u] = softmax(ss.pop(u), *updates[u])
        u = step - 2 * lag
        if 0 <= u < nu:
            pv[u] = weighted_values(mp[u][2], *updates[u])
        u = step - 3 * lag
        if 0 <= u < nu:
            hd, r0, nr = updates[u][:3]
            m_old, m_new, _ = mp.pop(u)
            done[u] = finish(m_old, m_new, pv.pop(u), *updates[u])
            for rr in range(r0, r0 + nr, hq):
                written_at[(hd, rr)] = step
            if updates[u][7] and hd == 1:
                o_ref[r0:r0 + nr, :] = jnp.where(lo, done.pop(u - 1), done.pop(u)).astype(BF16)


def _mla_call(q, k, v, B):
    T = q.shape[0]
    blk = pl.BlockSpec((SEQ, 2 * HEAD_PAD), lambda b, hp: (b, hp))
    return pl.pallas_call(
        _mla_kernel,
        grid=(B, MLA_HEADS // 2),
        in_specs=[blk, blk, blk],
        out_specs=pl.BlockSpec((SEQ, LANES), lambda b, hp: (b, hp)),
        out_shape=jax.ShapeDtypeStruct((T, MLA_HEADS * MLA_V), BF16),
        scratch_shapes=[pltpu.VMEM((2, SEQ, LANES), F32),
                        pltpu.VMEM((2, SEQ, LANES), F32)],
        compiler_params=pltpu.CompilerParams(dimension_semantics=("arbitrary", "arbitrary"),
                                             vmem_limit_bytes=VMEM_LIMIT),
        name="mla",
    )(q, k, v)


def _dil_kernel(bias_ref, q0_ref, k0_ref, v0_ref, q1_ref, k1_ref, v1_ref, q2_ref, k2_ref, v2_ref,
                o_ref, og_ref, lse_ref, nat_ref):
    lane = lax.broadcasted_iota(jnp.int32, (1, LANES), 1)
    lo = lane < HALF
    Q = Q_BLOCK
    ins = ((q0_ref, k0_ref, v0_ref), (q1_ref, k1_ref, v1_ref), (q2_ref, k2_ref, v2_ref))
    ones = jnp.ones((2 * Q, LANES), BF16)

    def rows_of(ref, g, r, a, b):
        return ref[a:b, :] if DIL_PATTERNS[g][1] == 1 else ref[r, a:b, :]

    def key_window(n):
        k0 = max(n - 1, 0) * Q
        return k0, (n + 1) * Q - k0

    def scores(g, r, n):
        k0, nk = key_window(n)
        q = rows_of(ins[g][0], g, r, n * Q, (n + 1) * Q)
        zero = jnp.zeros_like(q)
        qb = jnp.concatenate([jnp.where(lo, q, zero), jnp.where(lo, zero, q)], axis=0)
        return _dot_t(qb, rows_of(ins[g][1], g, r, k0, k0 + nk)) + bias_ref[g, :, 2 * Q - nk:]

    def softmax(s):
        m = jnp.max(s, axis=-1, keepdims=True)
        return m, jnp.exp2(s - m).astype(BF16)

    def weighted_values(g, r, n, p):
        k0, nk = key_window(n)
        vb = jnp.concatenate([rows_of(ins[g][2], g, r, k0, k0 + nk), ones[:nk]], axis=1)
        return _dot(p, vb)

    def finish(g, r, n, m, acc):
        d = DIL_PATTERNS[g][1]
        mb = jnp.broadcast_to(m, (2 * Q, LANES))
        den = jnp.where(lo, acc[:Q, LANES:], acc[Q:, LANES:])
        if d == 1:
            rows = slice(n * Q, (n + 1) * Q)
        elif d == MERGE_D:
            rows = slice(r * (SEQ // d) + n * Q, r * (SEQ // d) + (n + 1) * Q)
        else:
            rows = pl.ds((r % MERGE_D) * (SEQ // MERGE_D) + r // MERGE_D, Q, stride=MERGE_D)
        og_ref[g, rows, :] = jnp.where(lo, acc[:Q, :LANES], acc[Q:, :LANES]) / den
        lse_ref[g, rows, :] = jnp.where(lo, mb[:Q], mb[Q:]) + jnp.log2(den)

    blocks = [(g, r, n) for g, (_, d) in enumerate(DIL_PATTERNS)
              for r in range(d) for n in range(SEQ // d // Q)]
    nb = len(blocks)
    ss, mp, accs = {}, {}, {}
    for step in range(nb + 3 * DIL_LAG):
        if step < nb:
            ss[step] = scores(*blocks[step])
        b = step - DIL_LAG
        if 0 <= b < nb:
            mp[b] = softmax(ss.pop(b))
        b = step - 2 * DIL_LAG
        if 0 <= b < nb:
            accs[b] = weighted_values(*blocks[b], mp[b][1])
        b = step - 3 * DIL_LAG
        if 0 <= b < nb:
            finish(*blocks[b], mp.pop(b)[0], accs.pop(b))


    chunk = 256
    per_class = SEQ // MERGE_D
    for c in range(SEQ // chunk):
        r0, i0 = divmod(c * chunk, per_class)
        cm = slice(c * chunk, (c + 1) * chunk)
        nat = pl.ds(MERGE_D * i0 + r0, chunk, stride=MERGE_D)
        rows = (nat, cm, cm)
        ls = [lse_ref[g, rows[g], :] for g in range(DIL_GROUPS)]
        mx = jnp.maximum(jnp.maximum(ls[0], ls[1]), ls[2])
        num = jnp.zeros((chunk, LANES), F32)
        den = jnp.zeros((chunk, LANES), F32)
        for g in range(DIL_GROUPS):
            w = jnp.exp2(ls[g] - mx)
            num = num + w * og_ref[g, rows[g], :]
            den = den + w
        nat_ref[nat, :] = num / den
    o_ref[...] = nat_ref[...].astype(BF16)


def _dil_call(bias, dil_in, B):
    T = B * SEQ
    nhp = DIL_HEADS // 2
    in_specs = [pl.BlockSpec((DIL_GROUPS, None, 2 * Q_BLOCK, 2 * Q_BLOCK), lambda b, hp: (0, hp, 0, 0))]
    for _, d in DIL_PATTERNS:
        if d == 1:
            spec = pl.BlockSpec((SEQ, LANES), lambda b, hp: (b, hp))
        else:
            spec = pl.BlockSpec((None, d, SEQ // d, LANES), lambda b, hp: (b, 0, 0, hp))
        in_specs += [spec] * 3
    return pl.pallas_call(
        _dil_kernel,
        grid=(B, nhp),
        in_specs=in_specs,
        out_specs=pl.BlockSpec((SEQ, LANES), lambda b, hp: (b, hp)),
        out_shape=jax.ShapeDtypeStruct((T, DIL_WIDTH), BF16),
        scratch_shapes=[pltpu.VMEM((DIL_GROUPS, SEQ, LANES), F32),
                        pltpu.VMEM((DIL_GROUPS, SEQ, LANES), F32),
                        pltpu.VMEM((SEQ, LANES), F32)],
        compiler_params=pltpu.CompilerParams(dimension_semantics=("arbitrary", "arbitrary"),
                                             vmem_limit_bytes=VMEM_LIMIT),
        name="dil",
    )(bias, *dil_in)


def _out_kernel(x_ref, ob_ref, oc_ref, ng_ref, wa_ref, wbz_ref, wcz_ref,
                wg0_ref, wg1_ref, wg2_ref, wg3_ref, wg4_ref, wg5_ref, bg_ref, cw_ref, cb_ref,
                woa_ref, wob_ref, woc_ref, wo_ref, out_ref, ubuf):
    wg_refs = (wg0_ref, wg1_ref, wg2_ref, wg3_ref, wg4_ref, wg5_ref)

    def gate_pre(b):
        return jnp.concatenate([_dot(h, wg_refs[2 * b][...]), _dot(h, wg_refs[2 * b + 1][...])], axis=1)

    tm = ROW_TILE
    i = pl.program_id(0)
    x = x_ref[...]
    h = (x * _rms_scale(x, D_MODEL) * ng_ref[...]).astype(BF16)

    @pl.when(i % (SEQ // tm) == 0)
    def _():
        ubuf[0:8, :] = jnp.zeros((8, CONV_WIDTH), F32)

    W = CONV_WIDTH
    u = _dot(h, wa_ref[:, W:2 * W]) * _dot(h, wa_ref[:, 2 * W:3 * W])
    ubuf[8:8 + tm, :] = u
    conv = (cb_ref[...] + ubuf[6:6 + tm, :] * cw_ref[0:1, :] + ubuf[7:7 + tm, :] * cw_ref[1:2, :]
            + u * cw_ref[2:3, :])
    ubuf[0:8, :] = u[tm - 8:tm, :]
    az = _dot(h, wa_ref[:, 3 * W:4 * W])
    ya = _dot(h, wa_ref[:, 0:W]) * conv * (az * _sigmoid(az))

    D = D_MODEL
    ga = _sigmoid(gate_pre(0) + bg_ref[:, 0:D])
    merged = ga * _dot(ya.astype(BF16), woa_ref[...])

    zb = _dot(h, wbz_ref[...])
    yb = ob_ref[...].astype(F32) * (zb * _sigmoid(zb))
    gb = _sigmoid(gate_pre(1) + bg_ref[:, D:2 * D])
    merged = merged + gb * _dot(yb.astype(BF16), wob_ref[...])

    zc = _dot(h, wcz_ref[...])
    yc = oc_ref[...].astype(F32) * (zc * _sigmoid(zc))
    gc = _sigmoid(gate_pre(2) + bg_ref[:, 2 * D:3 * D])
    merged = merged + gc * _dot(yc.astype(BF16), woc_ref[...])

    out_ref[...] = x + _dot(merged.astype(BF16), wo_ref[...])


def _out_call(x2, ob, oc, ow, wfirst, wrest, l):
    T = x2.shape[0]
    tm = ROW_TILE
    row = lambda n: pl.BlockSpec((tm, n), lambda i: (i, 0))
    consts = [ow['bg'], ow['cw'], ow['cb'], ow['woa'], ow['wob'], ow['woc'], ow['wo']]
    rb = lambda off: (off - _O_BZ) // W_BLOCK
    gate_blocks = range(rb(_O_GATE), rb(_O_END))
    wspecs = ([_w_spec(l, 4 * CONV_WIDTH, 0), _w_spec(l, W_BLOCK, rb(_O_BZ)), _w_spec(l, W_BLOCK, rb(_O_CZ))]
              + [_w_spec(l, W_BLOCK, b) for b in gate_blocks])
    wargs = [wfirst] + [wrest] * (2 + len(gate_blocks))
    return pl.pallas_call(
        _out_kernel,
        grid=(T // tm,),
        in_specs=[row(D_MODEL), row(MLA_HEADS * MLA_V), row(DIL_WIDTH), _const_spec(ow['ng'].shape)]
                 + wspecs + [_const_spec(a.shape) for a in consts],
        out_specs=row(D_MODEL),
        out_shape=jax.ShapeDtypeStruct((T, D_MODEL), F32),
        scratch_shapes=[pltpu.VMEM((tm + 8, CONV_WIDTH), F32)],
        compiler_params=pltpu.CompilerParams(dimension_semantics=("arbitrary",),
                                             vmem_limit_bytes=VMEM_LIMIT),
        name="out",
    )(x2, ob, oc, ow['ng'], *wargs, *consts)


def _rope_tables():
    inv = ROPE_THETA ** (-jnp.arange(0, MLA_ROPE, 2, dtype=F32) / MLA_ROPE)
    ang = jnp.arange(SEQ, dtype=F32)[:, None] * inv[None, :]
    return jnp.cos(ang), jnp.sin(ang)


def _head_tables(g, cos, sin, scale):
    hr = MLA_ROPE // 2
    g1, g2 = g[MLA_NOPE:MLA_NOPE + hr], g[MLA_NOPE + hr:MLA_QK]
    ones = jnp.ones((SEQ, 1), F32)
    zpad = jnp.zeros((SEQ, HEAD_PAD - MLA_QK), F32)
    gc = jnp.concatenate([ones * g[None, :MLA_NOPE], cos * g1[None], cos * g2[None], zpad], axis=1)
    gs = jnp.concatenate([jnp.zeros((SEQ, MLA_NOPE), F32), -sin * g2[None], sin * g1[None], zpad], axis=1)
    return gc * scale, gs * scale


def _dil_bias():
    n = DIL_GROUPS * DIL_HEADS
    slopes = (2.0 ** (-8.0 * jnp.arange(1, n + 1, dtype=F32) / n)).reshape(DIL_GROUPS, DIL_HEADS)
    qq = jnp.arange(Q_BLOCK)[:, None]
    kk = jnp.arange(2 * Q_BLOCK)[None, :]
    j = Q_BLOCK + qq - kk
    tabs = []
    for gi, (window, d) in enumerate(DIL_PATTERNS):
        valid = (j >= 0) & (j <= window // d)
        dist = (d * j).astype(F32)
        tabs.append(jnp.where(valid[None], (-LOG2E) * slopes[gi][:, None, None] * dist[None], NEG))
    tab = jnp.stack(tabs, axis=0)
    return tab.reshape(DIL_GROUPS, DIL_HEADS // 2, 2 * Q_BLOCK, 2 * Q_BLOCK)


def _layer_params(l, p, cos, sin):
    zc = lambda n: jnp.zeros((D_MODEL, n), F32)
    wkpe = jnp.concatenate([zc(MLA_NOPE), p['w_in'][l, :, _O_KPE:_O_BZ], zc(HEAD_PAD - MLA_QK)], axis=1)

    wuq = p['w_uq'][l].reshape(MLA_Q_LORA, MLA_HEADS, MLA_QK)
    zq = jnp.zeros((MLA_Q_LORA, MLA_HEADS, HEAD_PAD - MLA_QK), F32)
    wuq_p = jnp.concatenate([wuq, zq], axis=-1).reshape(MLA_Q_LORA, MLA_HEADS * HEAD_PAD)

    wukv = p['w_ukv'][l].reshape(MLA_KV_LORA, MLA_HEADS, MLA_NOPE + MLA_V)
    zk = jnp.zeros((MLA_KV_LORA, MLA_HEADS, HALF), F32)
    wk = jnp.concatenate([wukv[..., :MLA_NOPE], zk], axis=-1).reshape(MLA_KV_LORA, MLA_HEADS * HEAD_PAD)
    wv4 = wukv[..., MLA_NOPE:].reshape(MLA_KV_LORA, MLA_HEADS // 2, 2, MLA_V)
    zv = jnp.zeros((MLA_KV_LORA, MLA_HEADS // 2, MLA_V), F32)
    wv = jnp.stack([jnp.concatenate([wv4[:, :, 0], zv], axis=-1),
                    jnp.concatenate([zv, wv4[:, :, 1]], axis=-1)], axis=2)
    wv = wv.reshape(MLA_KV_LORA, MLA_HEADS * HEAD_PAD)
    half_pat = jnp.concatenate([jnp.zeros((HALF,), F32), jnp.ones((HALF,), F32)])
    vone = jnp.concatenate([half_pat, 1.0 - half_pat] * (MLA_HEADS // 2))[None, :]

    gcq, gsq = _head_tables(p['mla_q_norm_g'][l], cos, sin, MLA_QK ** -0.5 * LOG2E)
    gck, gsk = _head_tables(p['mla_k_norm_g'][l], cos, sin, 1.0)

    gdq = (jnp.tile(p['dil_q_norm_g'][l][:, None, :], (1, DIL_HEADS, 1)).reshape(1, -1)
           * (DIL_HEAD_DIM ** -0.5 * LOG2E))
    gdk = jnp.tile(p['dil_k_norm_g'][l][:, None, :], (1, DIL_HEADS, 1)).reshape(1, -1)
    ii = jnp.arange(MXU_N) // DIL_HEAD_DIM
    bd = (ii[:, None] == ii[None, :]).astype(BF16)

    ng = p['norm_g'][l][None, :]
    prep = dict(ng=ng, wkpe=wkpe.astype(BF16), gqa=p['q_a_norm_g'][l][None, :],
                gkva=p['kv_a_norm_g'][l][None, :], wuq=wuq_p.astype(BF16),
                wk=wk.astype(BF16), wv=wv.astype(BF16), vone=vone, gcq=gcq, gsq=gsq, gck=gck, gsk=gsk,
                gdq=gdq, gdk=gdk, bd=bd)
    out = dict(ng=ng, bg=p['b_gate'][l][None, :],
               cw=p['conv_w'][l], cb=p['conv_b'][l][None, :],
               woa=p['w_out_a'][l].astype(BF16), wob=p['w_out_b'][l].astype(BF16),
               woc=p['w_out_c'][l].astype(BF16), wo=p['w_o'][l].astype(BF16))
    return prep, out


def kernel(x, norm_g, w_in, b_gate, conv_w, conv_b, q_a_norm_g, w_uq, kv_a_norm_g, w_ukv, mla_q_norm_g,
           mla_k_norm_g, dil_q_norm_g, dil_k_norm_g, w_out_a, w_out_b, w_out_c, w_o):
    B, S, D = x.shape
    assert S == SEQ and D == D_MODEL
    T = B * S
    p = dict(norm_g=norm_g, w_in=w_in, b_gate=b_gate, conv_w=conv_w, conv_b=conv_b,
             q_a_norm_g=q_a_norm_g, w_uq=w_uq, kv_a_norm_g=kv_a_norm_g, w_ukv=w_ukv,
             mla_q_norm_g=mla_q_norm_g, mla_k_norm_g=mla_k_norm_g, dil_q_norm_g=dil_q_norm_g,
             dil_k_norm_g=dil_k_norm_g, w_out_a=w_out_a, w_out_b=w_out_b, w_out_c=w_out_c, w_o=w_o)
    cos, sin = _rope_tables()
    bias = _dil_bias()
    wfirst = w_in[:, :, :_O_BZ].astype(BF16)
    wrest = w_in[:, :, _O_BZ:].astype(BF16)
    x2 = x.reshape(T, D)
    for l in range(DEPTH):
        pw, ow = _layer_params(l, p, cos, sin)
        q, k, v, *dil_in = _prep_call(x2, pw, wfirst, wrest, l, B)
        ob = _mla_call(q, k, v, B)
        oc = _dil_call(bias, dil_in, B)
        x2 = _out_call(x2, ob, oc, ow, wfirst, wrest, l)
    return x2.reshape(B, S, D)
```

```python
import numpy as np
import jax
import jax.numpy as jnp
from jax import lax
from jax.experimental import pallas as pl
from jax.experimental.pallas import tpu as pltpu

D_MODEL = 1024
SEQ = 2048
DEPTH = 2
CONV_WIDTH = 512
CONV_K = 3
MLA_HEADS = 8
MLA_Q_LORA = 256
MLA_KV_LORA = 128
MLA_NOPE = 64
MLA_ROPE = 32
MLA_V = 64
MLA_QK = MLA_NOPE + MLA_ROPE
ROPE_THETA = 10000.0
DIL_PATTERNS = ((128, 1), (512, 4), (2048, 16))
DIL_GROUPS = len(DIL_PATTERNS)
DIL_HEADS = 8
DIL_HEAD_DIM = 64
DIL_WIDTH = DIL_HEADS * DIL_HEAD_DIM
N_BRANCH = 3
Q_BLOCK = 128
EPS = 1e-6

SPLIT_SIZES = ((CONV_WIDTH,) * 4
               + (MLA_Q_LORA, MLA_KV_LORA, MLA_ROPE, MLA_HEADS * MLA_V)
               + (DIL_GROUPS * DIL_WIDTH,) * 3 + (DIL_WIDTH,)
               + (N_BRANCH * D_MODEL,))
_OFFS = tuple(int(v) for v in np.cumsum((0,) + SPLIT_SIZES))
(_O_AB, _O_AC, _O_AX, _O_AZ, _O_CQ, _O_CKV, _O_KPE, _O_BZ, _O_DQ, _O_DK, _O_DV, _O_CZ, _O_GATE,
 _O_END) = _OFFS

LANES = 128
HEAD_PAD = LANES
HALF = LANES // 2
MXU_N = 256
NEG = -1e30

BF16 = jnp.bfloat16
F32 = jnp.float32

ROW_TILE = 512
MLA_T = 512
N_DSCR = 4
W_BLOCK = 512
MERGE_D = 4
PREP_INTERLEAVE = (2, 2, 1)
RELAYOUT_STRIDE = 4
HEAD_PIECES = 4
PREP_SSQ_LAG = 2
MLA_TALL = 1024
MLA_LAG = 1
DIL_LAG = 3
LOG2E = 1.4426950408889634
VMEM_LIMIT = 58 * 1024 * 1024


def _rms_scale(v, n):
    return lax.rsqrt(jnp.sum(v * v, axis=-1, keepdims=True) * (1.0 / n) + EPS)


def _sigmoid(v):
    return 1.0 / (1.0 + jnp.exp(-v))


def _dot(a, b):
    return jnp.dot(a, b, preferred_element_type=F32)


def _dot_t(a, b):
    return lax.dot_general(a, b, (((1,), (1,)), ((), ())), preferred_element_type=F32)


def _const_spec(shape):
    nd = len(shape)
    return pl.BlockSpec(shape, lambda *_: (0,) * nd, pipeline_mode=pl.Buffered(1))


def _prep_kernel(x_ref, ng_ref, wcq_ref, wckv_ref, wkpe_ref, gqa_ref, gkva_ref, wuq_ref, wk_ref, wv_ref,
                 vone_ref, gcq_ref, gsq_ref, gck_ref, gsk_ref, gdq_ref, gdk_ref, bd_ref, *rest):
    nwd = 3 * DIL_GROUPS * DIL_WIDTH // W_BLOCK
    wd_refs = rest[:nwd]
    (q_ref, k_ref, v_ref, dq0_ref, dk0_ref, dv0_ref, dq1_ref, dk1_ref, dv1_ref,
     dq2_ref, dk2_ref, dv2_ref, dscr, dscr2) = rest[nwd:]

    def wd_cols(col):
        return wd_refs[col // W_BLOCK][:, col % W_BLOCK:col % W_BLOCK + MXU_N]

    tm = ROW_TILE
    x = x_ref[...]
    h = (x * _rms_scale(x, D_MODEL) * ng_ref[...]).astype(BF16)

    nd = DIL_GROUPS * DIL_WIDTH
    bd = bd_ref[...]
    outs = ((dq0_ref, dk0_ref, dv0_ref), (dq1_ref, dk1_ref, dv1_ref), (dq2_ref, dk2_ref, dv2_ref))
    n_store = [0]

    def store(kind, c, val):
        g, half = divmod(c, DIL_WIDTH // MXU_N)
        sl = slice(MXU_N * half, MXU_N * (half + 1))
        d = DIL_PATTERNS[g][1]
        ref = outs[g][kind]
        if d == 1:
            ref[:, sl] = val.astype(BF16)
        else:
            slot = n_store[0] % N_DSCR
            n_store[0] += 1
            for cb in range(MXU_N // LANES):
                dscr[slot, cb] = val[:, LANES * cb:LANES * (cb + 1)]
            for cb in range(MXU_N // LANES):
                c0 = MXU_N * half + LANES * cb
                if d <= RELAYOUT_STRIDE:
                    for r in range(d):
                        ref[r, :, c0:c0 + LANES] = dscr[slot, cb, pl.ds(r, tm // d, stride=d), :].astype(BF16)
                else:
                    f = RELAYOUT_STRIDE
                    for r0 in range(f):
                        dscr2[slot, cb, r0 * (tm // f):(r0 + 1) * (tm // f), :] = (
                            dscr[slot, cb, pl.ds(r0, tm // f, stride=f), :])
                    for r0 in range(f):
                        for r1 in range(d // f):
                            ref[f * r1 + r0, :, c0:c0 + LANES] = dscr2[
                                slot, cb, pl.ds(r0 * (tm // f) + r1, tm // d, stride=d // f), :].astype(BF16)

    def finish(kind, c, t):
        g_ref = gdq_ref if kind == 0 else gdk_ref
        sl = slice(MXU_N * c, MXU_N * (c + 1))
        ssq = _dot((t * t).astype(BF16), bd)
        store(kind, c, t * lax.rsqrt(ssq * (1.0 / DIL_HEAD_DIM) + EPS) * g_ref[:, sl])

    pending = []
    nch = nd // MXU_N

    def dil_chunk(kind, c):
        t = _dot(h, wd_cols(kind * nd + MXU_N * c))
        if len(pending) >= PREP_SSQ_LAG:
            finish(*pending.pop(0))
        pending.append((kind, c, t))
        if kind == 1:
            store(2, c, _dot(h, wd_cols(2 * nd + MXU_N * c)))

    chunks = [(kind, c) for kind in range(2) for c in range(nch)]

    def dil_steps(n):
        for _ in range(n):
            if chunks:
                dil_chunk(*chunks.pop(0))


    cq = _dot(h, wcq_ref[...])
    lat = _dot(h, jnp.concatenate([wckv_ref[...], wkpe_ref[...]], axis=1))
    dil_steps(PREP_INTERLEAVE[0])
    ckv = lat[:, :MLA_KV_LORA]
    kpe = lat[:, MLA_KV_LORA:]
    cqn = (cq * _rms_scale(cq, MLA_Q_LORA) * gqa_ref[...]).astype(BF16)
    ckvn = (ckv * _rms_scale(ckv, MLA_KV_LORA) * gkva_ref[...]).astype(BF16)

    qraw = _dot(cqn, wuq_ref[...])
    kn = _dot(ckvn, wk_ref[...])
    vp = _dot(ckvn, wv_ref[...]) + vone_ref[...]
    v_ref[...] = vp.astype(BF16)
    dil_steps(PREP_INTERLEAVE[1])

    lane = lax.broadcasted_iota(jnp.int32, (1, LANES), 1)
    first_half = lane < MLA_NOPE + MLA_ROPE // 2

    def swap_halves(t):
        return jnp.where(first_half, pltpu.roll(t, LANES - MLA_ROPE // 2, 1),
                         pltpu.roll(t, MLA_ROPE // 2, 1))

    gcq = gcq_ref[...]
    gsq = gsq_ref[...]
    gck = gck_ref[...]
    krot = swap_halves(kpe) * gsk_ref[...]
    for hd in range(MLA_HEADS):
        sl = slice(HEAD_PAD * hd, HEAD_PAD * (hd + 1))
        qh = qraw[:, sl]
        q_ref[:, sl] = (_rms_scale(qh, MLA_QK) * (qh * gcq + swap_halves(qh) * gsq)).astype(BF16)
        kh = kn[:, sl] + kpe
        k_ref[:, sl] = (_rms_scale(kh, MLA_QK) * (kh * gck + krot)).astype(BF16)
        dil_steps(PREP_INTERLEAVE[2])

    dil_steps(len(chunks))
    while pending:
        finish(*pending.pop(0))


def _w_spec(l, width, idx):
    return pl.BlockSpec((None, D_MODEL, width), lambda *_: (l, 0, idx), pipeline_mode=pl.Buffered(1))


def _prep_call(x2, pw, wfirst, wrest, l, B):
    T = x2.shape[0]
    tm = ROW_TILE
    nst = SEQ // tm
    row = lambda n: pl.BlockSpec((tm, n), lambda i: (i, 0))
    tab = pl.BlockSpec((tm, LANES), lambda i: (i % nst, 0))
    consts = [pw['ng']]
    wspecs = [_w_spec(l, MLA_Q_LORA, _O_CQ // MLA_Q_LORA), _w_spec(l, MLA_KV_LORA, _O_CKV // MLA_KV_LORA)]
    consts1 = [pw['wkpe'], pw['gqa'], pw['gkva'], pw['wuq'], pw['wk'], pw['wv'], pw['vone']]
    tabs = [pw['gcq'], pw['gsq'], pw['gck'], pw['gsk']]
    consts2 = [pw['gdq'], pw['gdk'], pw['bd']]
    wd_blocks = range((_O_DQ - _O_BZ) // W_BLOCK, (_O_CZ - _O_BZ) // W_BLOCK)
    in_specs = ([row(D_MODEL)] + [_const_spec(a.shape) for a in consts] + wspecs
                + [_const_spec(a.shape) for a in consts1] + [tab] * 4
                + [_const_spec(a.shape) for a in consts2]
                + [_w_spec(l, W_BLOCK, b) for b in wd_blocks])
    args = ([x2] + consts + [wfirst, wfirst] + consts1 + tabs + consts2 + [wrest] * len(wd_blocks))
    out_shape = [jax.ShapeDtypeStruct((T, MLA_HEADS * HEAD_PAD), BF16)] * 3
    out_specs = [row(MLA_HEADS * HEAD_PAD)] * 3
    for _, d in DIL_PATTERNS:
        if d == 1:
            out_shape += [jax.ShapeDtypeStruct((T, DIL_WIDTH), BF16)] * 3
            out_specs += [row(DIL_WIDTH)] * 3
        else:
            out_shape += [jax.ShapeDtypeStruct((B, d, SEQ // d, DIL_WIDTH), BF16)] * 3
            out_specs += [pl.BlockSpec((None, d, tm // d, DIL_WIDTH),
                                       lambda i: (i // nst, 0, i % nst, 0))] * 3
    return pl.pallas_call(
        _prep_kernel,
        grid=(T // tm,),
        in_specs=in_specs,
        out_specs=out_specs,
        out_shape=out_shape,
        scratch_shapes=[pltpu.VMEM((N_DSCR, MXU_N // LANES, tm, LANES), F32),
                        pltpu.VMEM((N_DSCR, MXU_N // LANES, tm, LANES), F32)],
        compiler_params=pltpu.CompilerParams(dimension_semantics=("arbitrary",),
                                             vmem_limit_bytes=VMEM_LIMIT),
        name="prep",
    )(*args)


def _mla_kernel(q_ref, k_ref, v_ref, o_ref, acc_ref, m_ref):
    t = MLA_T
    nt = SEQ // t
    hq = t // 2
    lane = lax.broadcasted_iota(jnp.int32, (1, LANES), 1)
    lo = lane < HALF
    top_mask = (lax.broadcasted_iota(jnp.int32, (hq, hq), 1)
                <= lax.broadcasted_iota(jnp.int32, (hq, hq), 0))
    bot_mask = (lax.broadcasted_iota(jnp.int32, (hq, t), 1)
                <= lax.broadcasted_iota(jnp.int32, (hq, t), 0) + hq)

    updates = []
    for j in range(nt):
        for r0 in range((j + 1) * t, SEQ, MLA_TALL):
            for hd in range(2):
                updates.append((hd, r0, min(MLA_TALL, SEQ - r0), j * t, t, None, j == 0, False))
        for r0, nr, nk, mask in ((j * t, hq, hq, top_mask), (j * t + hq, hq, t, bot_mask)):
            for hd in range(2):
                updates.append((hd, r0, nr, j * t, nk, mask, j == 0, True))

    def scores(hd, r0, nr, k0, nk, mask, first, last):
        sl = slice(HEAD_PAD * hd, HEAD_PAD * (hd + 1))
        s = _dot_t(q_ref[r0:r0 + nr, sl], k_ref[k0:k0 + nk, sl])
        return s if mask is None else jnp.where(mask, s, NEG)

    def softmax(s, hd, r0, nr, k0, nk, mask, first, last):
        m_new = jnp.broadcast_to(jnp.max(s, axis=-1, keepdims=True), (nr, LANES))
        m_old = None
        if not first:
            m_old = m_ref[hd, r0:r0 + nr, :]
            m_new = jnp.maximum(m_old, m_new)
        p = jnp.exp2(s - jnp.concatenate([m_new] * (nk // LANES), axis=1)).astype(BF16)
        return m_old, m_new, p

    def weighted_values(p, hd, r0, nr, k0, nk, mask, first, last):
        return _dot(p, v_ref[k0:k0 + nk, HEAD_PAD * hd:HEAD_PAD * (hd + 1)])

    def finish(m_old, m_new, acc, hd, r0, nr, k0, nk, mask, first, last):
        if not first:
            acc = acc_ref[hd, r0:r0 + nr, :] * jnp.exp2(m_old - m_new) + acc
        if last:
            return acc / pltpu.roll(acc, HALF, 1)
        acc_ref[hd, r0:r0 + nr, :] = acc
        m_ref[hd, r0:r0 + nr, :] = m_new
        return None

    nu = len(updates)
    lag = MLA_LAG
    written_at = {}
    ss, mp, pv, done = {}, {}, {}, {}
    for step in range(nu + 3 * lag):
        if step < nu:
            ss[step] = scores(*updates[step])
        u = step - lag
        if 0 <= u < nu:
            hd, r0, nr = updates[u][:3]
            for rr in range(r0, r0 + nr, hq):
                assert updates[u][6] or written_at[(hd, rr)] < step, "softmax state read too early"
            mp[u] = softmax(ss.pop(u), *updates[u])
        u = step - 2 * lag
        if 0 <= u < nu:
            pv[u] = weighted_values(mp[u][2], *updates[u])
        u = step - 3 * lag
        if 0 <= u < nu:
            hd, r0, nr = updates[u][:3]
            m_old, m_new, _ = mp.pop(u)
            done[u] = finish(m_old, m_new, pv.pop(u), *updates[u])
            for rr in range(r0, r0 + nr, hq):
                written_at[(hd, rr)] = step
            if updates[u][7] and hd == 1:
                o_ref[r0:r0 + nr, :] = jnp.where(lo, done.pop(u - 1), done.pop(u)).astype(BF16)


def _mla_call(q, k, v, B):
    T = q.shape[0]
    blk = pl.BlockSpec((SEQ, 2 * HEAD_PAD), lambda b, hp: (b, hp))
    return pl.pallas_call(
        _mla_kernel,
        grid=(B, MLA_HEADS // 2),
        in_specs=[blk, blk, blk],
        out_specs=pl.BlockSpec((SEQ, LANES), lambda b, hp: (b, hp)),
        out_shape=jax.ShapeDtypeStruct((T, MLA_HEADS * MLA_V), BF16),
        scratch_shapes=[pltpu.VMEM((2, SEQ, LANES), F32),
                        pltpu.VMEM((2, SEQ, LANES), F32)],
        compiler_params=pltpu.CompilerParams(dimension_semantics=("arbitrary", "arbitrary"),
                                             vmem_limit_bytes=VMEM_LIMIT),
        name="mla",
    )(q, k, v)


def _dil_kernel(bias_ref, q0_ref, k0_ref, v0_ref, q1_ref, k1_ref, v1_ref, q2_ref, k2_ref, v2_ref,
                o_ref, og_ref, lse_ref, nat_ref):
    lane = lax.broadcasted_iota(jnp.int32, (1, LANES), 1)
    lo = lane < HALF
    Q = Q_BLOCK
    ins = ((q0_ref, k0_ref, v0_ref), (q1_ref, k1_ref, v1_ref), (q2_ref, k2_ref, v2_ref))
    ones = jnp.ones((2 * Q, LANES), BF16)

    def rows_of(ref, g, r, a, b):
        return ref[a:b, :] if DIL_PATTERNS[g][1] == 1 else ref[r, a:b, :]

    def key_window(n):
        k0 = max(n - 1, 0) * Q
        return k0, (n + 1) * Q - k0

    def scores(g, r, n):
        k0, nk = key_window(n)
        q = rows_of(ins[g][0], g, r, n * Q, (n + 1) * Q)
        zero = jnp.zeros_like(q)
        qb = jnp.concatenate([jnp.where(lo, q, zero), jnp.where(lo, zero, q)], axis=0)
        return _dot_t(qb, rows_of(ins[g][1], g, r, k0, k0 + nk)) + bias_ref[g, :, 2 * Q - nk:]

    def softmax(s):
        m = jnp.max(s, axis=-1, keepdims=True)
        return m, jnp.exp2(s - m).astype(BF16)

    def weighted_values(g, r, n, p):
        k0, nk = key_window(n)
        vb = jnp.concatenate([rows_of(ins[g][2], g, r, k0, k0 + nk), ones[:nk]], axis=1)
        return _dot(p, vb)

    def finish(g, r, n, m, acc):
        d = DIL_PATTERNS[g][1]
        mb = jnp.broadcast_to(m, (2 * Q, LANES))
        den = jnp.where(lo, acc[:Q, LANES:], acc[Q:, LANES:])
        if d == 1:
            rows = slice(n * Q, (n + 1) * Q)
        elif d == MERGE_D:
            rows = slice(r * (SEQ // d) + n * Q, r * (SEQ // d) + (n + 1) * Q)
        else:
            rows = pl.ds((r % MERGE_D) * (SEQ // MERGE_D) + r // MERGE_D, Q, stride=MERGE_D)
        og_ref[g, rows, :] = jnp.where(lo, acc[:Q, :LANES], acc[Q:, :LANES]) / den
        lse_ref[g, rows, :] = jnp.where(lo, mb[:Q], mb[Q:]) + jnp.log2(den)

    blocks = [(g, r, n) for g, (_, d) in enumerate(DIL_PATTERNS)
              for r in range(d) for n in range(SEQ // d // Q)]
    nb = len(blocks)
    ss, mp, accs = {}, {}, {}
    for step in range(nb + 3 * DIL_LAG):
        if step < nb:
            ss[step] = scores(*blocks[step])
        b = step - DIL_LAG
        if 0 <= b < nb:
            mp[b] = softmax(ss.pop(b))
        b = step - 2 * DIL_LAG
        if 0 <= b < nb:
            accs[b] = weighted_values(*blocks[b], mp[b][1])
        b = step - 3 * DIL_LAG
        if 0 <= b < nb:
            finish(*blocks[b], mp.pop(b)[0], accs.pop(b))


    chunk = 256
    per_class = SEQ // MERGE_D
    for c in range(SEQ // chunk):
        r0, i0 = divmod(c * chunk, per_class)
        cm = slice(c * chunk, (c + 1) * chunk)
        nat = pl.ds(MERGE_D * i0 + r0, chunk, stride=MERGE_D)
        rows = (nat, cm, cm)
        ls = [lse_ref[g, rows[g], :] for g in range(DIL_GROUPS)]
        mx = jnp.maximum(jnp.maximum(ls[0], ls[1]), ls[2])
        num = jnp.zeros((chunk, LANES), F32)
        den = jnp.zeros((chunk, LANES), F32)
        for g in range(DIL_GROUPS):
            w = jnp.exp2(ls[g] - mx)
            num = num + w * og_ref[g, rows[g], :]
            den = den + w
        nat_ref[nat, :] = num / den
    o_ref[...] = nat_ref[...].astype(BF16)


def _dil_call(bias, dil_in, B):
    T = B * SEQ
    nhp = DIL_HEADS // 2
    in_specs = [pl.BlockSpec((DIL_GROUPS, None, 2 * Q_BLOCK, 2 * Q_BLOCK), lambda b, hp: (0, hp, 0, 0))]
    for _, d in DIL_PATTERNS:
        if d == 1:
            spec = pl.BlockSpec((SEQ, LANES), lambda b, hp: (b, hp))
        else:
            spec = pl.BlockSpec((None, d, SEQ // d, LANES), lambda b, hp: (b, 0, 0, hp))
        in_specs += [spec] * 3
    return pl.pallas_call(
        _dil_kernel,
        grid=(B, nhp),
        in_specs=in_specs,
        out_specs=pl.BlockSpec((SEQ, LANES), lambda b, hp: (b, hp)),
        out_shape=jax.ShapeDtypeStruct((T, DIL_WIDTH), BF16),
        scratch_shapes=[pltpu.VMEM((DIL_GROUPS, SEQ, LANES), F32),
                        pltpu.VMEM((DIL_GROUPS, SEQ, LANES), F32),
                        pltpu.VMEM((SEQ, LANES), F32)],
        compiler_params=pltpu.CompilerParams(dimension_semantics=("arbitrary", "arbitrary"),
                                             vmem_limit_bytes=VMEM_LIMIT),
        name="dil",
    )(bias, *dil_in)


def _out_kernel(x_ref, ob_ref, oc_ref, ng_ref, wa_ref, wbz_ref, wcz_ref,
                wg0_ref, wg1_ref, wg2_ref, wg3_ref, wg4_ref, wg5_ref, bg_ref, cw_ref, cb_ref,
                woa_ref, wob_ref, woc_ref, wo_ref, out_ref, ubuf):
    wg_refs = (wg0_ref, wg1_ref, wg2_ref, wg3_ref, wg4_ref, wg5_ref)

    def gate_pre(b):
        return jnp.concatenate([_dot(h, wg_refs[2 * b][...]), _dot(h, wg_refs[2 * b + 1][...])], axis=1)

    tm = ROW_TILE
    i = pl.program_id(0)
    @pl.when(i % (SEQ // tm) == 0)
    def _():
        ubuf[0:8, :] = jnp.zeros((8, CONV_WIDTH), F32)

    W = CONV_WIDTH
    hs, us = [], []
    for pc in range(HEAD_PIECES):
        xp = x_ref[pc * (tm // HEAD_PIECES):(pc + 1) * (tm // HEAD_PIECES), :]
        hp = (xp * _rms_scale(xp, D_MODEL) * ng_ref[...]).astype(BF16)
        hs.append(hp)
        us.append(_dot(hp, wa_ref[:, W:2 * W]))
    h = jnp.concatenate(hs, axis=0)
    x = x_ref[...]

    u = jnp.concatenate(us, axis=0) * _dot(h, wa_ref[:, 2 * W:3 * W])
    ubuf[8:8 + tm, :] = u
    conv = (cb_ref[...] + ubuf[6:6 + tm, :] * cw_ref[0:1, :] + ubuf[7:7 + tm, :] * cw_ref[1:2, :]
            + u * cw_ref[2:3, :])
    ubuf[0:8, :] = u[tm - 8:tm, :]
    az = _dot(h, wa_ref[:, 3 * W:4 * W])
    ya = _dot(h, wa_ref[:, 0:W]) * conv * (az * _sigmoid(az))

    D = D_MODEL
    ga = _sigmoid(gate_pre(0) + bg_ref[:, 0:D])
    merged = ga * _dot(ya.astype(BF16), woa_ref[...])

    zb = _dot(h, wbz_ref[...])
    yb = ob_ref[...].astype(F32) * (zb * _sigmoid(zb))
    gb = _sigmoid(gate_pre(1) + bg_ref[:, D:2 * D])
    merged = merged + gb * _dot(yb.astype(BF16), wob_ref[...])

    zc = _dot(h, wcz_ref[...])
    yc = oc_ref[...].astype(F32) * (zc * _sigmoid(zc))
    gc = _sigmoid(gate_pre(2) + bg_ref[:, 2 * D:3 * D])
    merged = merged + gc * _dot(yc.astype(BF16), woc_ref[...])

    out_ref[...] = x + _dot(merged.astype(BF16), wo_ref[...])


def _out_call(x2, ob, oc, ow, wfirst, wrest, l):
    T = x2.shape[0]
    tm = ROW_TILE
    row = lambda n: pl.BlockSpec((tm, n), lambda i: (i, 0))
    consts = [ow['bg'], ow['cw'], ow['cb'], ow['woa'], ow['wob'], ow['woc'], ow['wo']]
    rb = lambda off: (off - _O_BZ) // W_BLOCK
    gate_blocks = range(rb(_O_GATE), rb(_O_END))
    wspecs = ([_w_spec(l, 4 * CONV_WIDTH, 0), _w_spec(l, W_BLOCK, rb(_O_BZ)), _w_spec(l, W_BLOCK, rb(_O_CZ))]
              + [_w_spec(l, W_BLOCK, b) for b in gate_blocks])
    wargs = [wfirst] + [wrest] * (2 + len(gate_blocks))
    return pl.pallas_call(
        _out_kernel,
        grid=(T // tm,),
        in_specs=[row(D_MODEL), row(MLA_HEADS * MLA_V), row(DIL_WIDTH), _const_spec(ow['ng'].shape)]
                 + wspecs + [_const_spec(a.shape) for a in consts],
        out_specs=row(D_MODEL),
        out_shape=jax.ShapeDtypeStruct((T, D_MODEL), F32),
        scratch_shapes=[pltpu.VMEM((tm + 8, CONV_WIDTH), F32)],
        compiler_params=pltpu.CompilerParams(dimension_semantics=("arbitrary",),
                                             vmem_limit_bytes=VMEM_LIMIT),
        name="out",
    )(x2, ob, oc, ow['ng'], *wargs, *consts)


def _rope_tables():
    inv = ROPE_THETA ** (-jnp.arange(0, MLA_ROPE, 2, dtype=F32) / MLA_ROPE)
    ang = jnp.arange(SEQ, dtype=F32)[:, None] * inv[None, :]
    return jnp.cos(ang), jnp.sin(ang)


def _head_tables(g, cos, sin, scale):
    hr = MLA_ROPE // 2
    g1, g2 = g[MLA_NOPE:MLA_NOPE + hr], g[MLA_NOPE + hr:MLA_QK]
    ones = jnp.ones((SEQ, 1), F32)
    zpad = jnp.zeros((SEQ, HEAD_PAD - MLA_QK), F32)
    gc = jnp.concatenate([ones * g[None, :MLA_NOPE], cos * g1[None], cos * g2[None], zpad], axis=1)
    gs = jnp.concatenate([jnp.zeros((SEQ, MLA_NOPE), F32), -sin * g2[None], sin * g1[None], zpad], axis=1)
    return gc * scale, gs * scale


def _dil_bias():
    n = DIL_GROUPS * DIL_HEADS
    slopes = (2.0 ** (-8.0 * jnp.arange(1, n + 1, dtype=F32) / n)).reshape(DIL_GROUPS, DIL_HEADS)
    qq = jnp.arange(Q_BLOCK)[:, None]
    kk = jnp.arange(2 * Q_BLOCK)[None, :]
    j = Q_BLOCK + qq - kk
    tabs = []
    for gi, (window, d) in enumerate(DIL_PATTERNS):
        valid = (j >= 0) & (j <= window // d)
        dist = (d * j).astype(F32)
        tabs.append(jnp.where(valid[None], (-LOG2E) * slopes[gi][:, None, None] * dist[None], NEG))
    tab = jnp.stack(tabs, axis=0)
    return tab.reshape(DIL_GROUPS, DIL_HEADS // 2, 2 * Q_BLOCK, 2 * Q_BLOCK)


def _layer_params(l, p, cos, sin):
    zc = lambda n: jnp.zeros((D_MODEL, n), F32)
    wkpe = jnp.concatenate([zc(MLA_NOPE), p['w_in'][l, :, _O_KPE:_O_BZ], zc(HEAD_PAD - MLA_QK)], axis=1)

    wuq = p['w_uq'][l].reshape(MLA_Q_LORA, MLA_HEADS, MLA_QK)
    zq = jnp.zeros((MLA_Q_LORA, MLA_HEADS, HEAD_PAD - MLA_QK), F32)
    wuq_p = jnp.concatenate([wuq, zq], axis=-1).reshape(MLA_Q_LORA, MLA_HEADS * HEAD_PAD)

    wukv = p['w_ukv'][l].reshape(MLA_KV_LORA, MLA_HEADS, MLA_NOPE + MLA_V)
    zk = jnp.zeros((MLA_KV_LORA, MLA_HEADS, HALF), F32)
    wk = jnp.concatenate([wukv[..., :MLA_NOPE], zk], axis=-1).reshape(MLA_KV_LORA, MLA_HEADS * HEAD_PAD)
    wv4 = wukv[..., MLA_NOPE:].reshape(MLA_KV_LORA, MLA_HEADS // 2, 2, MLA_V)
    zv = jnp.zeros((MLA_KV_LORA, MLA_HEADS // 2, MLA_V), F32)
    wv = jnp.stack([jnp.concatenate([wv4[:, :, 0], zv], axis=-1),
                    jnp.concatenate([zv, wv4[:, :, 1]], axis=-1)], axis=2)
    wv = wv.reshape(MLA_KV_LORA, MLA_HEADS * HEAD_PAD)
    half_pat = jnp.concatenate([jnp.zeros((HALF,), F32), jnp.ones((HALF,), F32)])
    vone = jnp.concatenate([half_pat, 1.0 - half_pat] * (MLA_HEADS // 2))[None, :]

    gcq, gsq = _head_tables(p['mla_q_norm_g'][l], cos, sin, MLA_QK ** -0.5 * LOG2E)
    gck, gsk = _head_tables(p['mla_k_norm_g'][l], cos, sin, 1.0)

    gdq = (jnp.tile(p['dil_q_norm_g'][l][:, None, :], (1, DIL_HEADS, 1)).reshape(1, -1)
           * (DIL_HEAD_DIM ** -0.5 * LOG2E))
    gdk = jnp.tile(p['dil_k_norm_g'][l][:, None, :], (1, DIL_HEADS, 1)).reshape(1, -1)
    ii = jnp.arange(MXU_N) // DIL_HEAD_DIM
    bd = (ii[:, None] == ii[None, :]).astype(BF16)

    ng = p['norm_g'][l][None, :]
    prep = dict(ng=ng, wkpe=wkpe.astype(BF16), gqa=p['q_a_norm_g'][l][None, :],
                gkva=p['kv_a_norm_g'][l][None, :], wuq=wuq_p.astype(BF16),
                wk=wk.astype(BF16), wv=wv.astype(BF16), vone=vone, gcq=gcq, gsq=gsq, gck=gck, gsk=gsk,
                gdq=gdq, gdk=gdk, bd=bd)
    out = dict(ng=ng, bg=p['b_gate'][l][None, :],
               cw=p['conv_w'][l], cb=p['conv_b'][l][None, :],
               woa=p['w_out_a'][l].astype(BF16), wob=p['w_out_b'][l].astype(BF16),
               woc=p['w_out_c'][l].astype(BF16), wo=p['w_o'][l].astype(BF16))
    return prep, out


def kernel(x, norm_g, w_in, b_gate, conv_w, conv_b, q_a_norm_g, w_uq, kv_a_norm_g, w_ukv, mla_q_norm_g,
           mla_k_norm_g, dil_q_norm_g, dil_k_norm_g, w_out_a, w_out_b, w_out_c, w_o):
    B, S, D = x.shape
    assert S == SEQ and D == D_MODEL
    T = B * S
    p = dict(norm_g=norm_g, w_in=w_in, b_gate=b_gate, conv_w=conv_w, conv_b=conv_b,
             q_a_norm_g=q_a_norm_g, w_uq=w_uq, kv_a_norm_g=kv_a_norm_g, w_ukv=w_ukv,
             mla_q_norm_g=mla_q_norm_g, mla_k_norm_g=mla_k_norm_g, dil_q_norm_g=dil_q_norm_g,
             dil_k_norm_g=dil_k_norm_g, w_out_a=w_out_a, w_out_b=w_out_b, w_out_c=w_out_c, w_o=w_o)
    cos, sin = _rope_tables()
    bias = _dil_bias()
    wfirst = w_in[:, :, :_O_BZ].astype(BF16)
    wrest = w_in[:, :, _O_BZ:].astype(BF16)
    x2 = x.reshape(T, D)
    for l in range(DEPTH):
        pw, ow = _layer_params(l, p, cos, sin)
        q, k, v, *dil_in = _prep_call(x2, pw, wfirst, wrest, l, B)
        ob = _mla_call(q, k, v, B)
        oc = _dil_call(bias, dil_in, B)
        x2 = _out_call(x2, ob, oc, ow, wfirst, wrest, l)
    return x2.reshape(B, S, D)
```

```python
import numpy as np
import jax
import jax.numpy as jnp
from jax import lax
from jax.experimental import pallas as pl
from jax.experimental.pallas import tpu as pltpu

D_MODEL = 1024
SEQ = 2048
DEPTH = 2
CONV_WIDTH = 512
CONV_K = 3
MLA_HEADS = 8
MLA_Q_LORA = 256
MLA_KV_LORA = 128
MLA_NOPE = 64
MLA_ROPE = 32
MLA_V = 64
MLA_QK = MLA_NOPE + MLA_ROPE
ROPE_THETA = 10000.0
DIL_PATTERNS = ((128, 1), (512, 4), (2048, 16))
DIL_GROUPS = len(DIL_PATTERNS)
DIL_HEADS = 8
DIL_HEAD_DIM = 64
DIL_WIDTH = DIL_HEADS * DIL_HEAD_DIM
N_BRANCH = 3
Q_BLOCK = 128
EPS = 1e-6

SPLIT_SIZES = ((CONV_WIDTH,) * 4
               + (MLA_Q_LORA, MLA_KV_LORA, MLA_ROPE, MLA_HEADS * MLA_V)
               + (DIL_GROUPS * DIL_WIDTH,) * 3 + (DIL_WIDTH,)
               + (N_BRANCH * D_MODEL,))
_OFFS = tuple(int(v) for v in np.cumsum((0,) + SPLIT_SIZES))
(_O_AB, _O_AC, _O_AX, _O_AZ, _O_CQ, _O_CKV, _O_KPE, _O_BZ, _O_DQ, _O_DK, _O_DV, _O_CZ, _O_GATE,
 _O_END) = _OFFS

LANES = 128
HEAD_PAD = LANES
HALF = LANES // 2
MXU_N = 256
NEG = -1e30

BF16 = jnp.bfloat16
F32 = jnp.float32

ROW_TILE = 512
MLA_T = 512
N_DSCR = 4
W_BLOCK = 512
MERGE_D = 4
PREP_INTERLEAVE = (2, 2, 1)
RELAYOUT_STRIDE = 4
HEAD_PIECES = 4
PREP_SSQ_LAG = 2
MLA_TALL = 1024
MLA_LAG = (1, 1, 1)
DIL_LAG = (2, 3, 1)
LOG2E = 1.4426950408889634
VMEM_LIMIT = 58 * 1024 * 1024


def _rms_scale(v, n):
    return lax.rsqrt(jnp.sum(v * v, axis=-1, keepdims=True) * (1.0 / n) + EPS)


def _sigmoid(v):
    return 1.0 / (1.0 + jnp.exp(-v))


def _dot(a, b):
    return jnp.dot(a, b, preferred_element_type=F32)


def _dot_t(a, b):
    return lax.dot_general(a, b, (((1,), (1,)), ((), ())), preferred_element_type=F32)


def _const_spec(shape):
    nd = len(shape)
    return pl.BlockSpec(shape, lambda *_: (0,) * nd, pipeline_mode=pl.Buffered(1))


def _prep_kernel(x_ref, ng_ref, wcq_ref, wckv_ref, wkpe_ref, gqa_ref, gkva_ref, wuq_ref, wk_ref, wv_ref,
                 vone_ref, gcq_ref, gsq_ref, gck_ref, gsk_ref, gdq_ref, gdk_ref, bd_ref, *rest):
    nwd = 3 * DIL_GROUPS * DIL_WIDTH // W_BLOCK
    wd_refs = rest[:nwd]
    (q_ref, k_ref, v_ref, dq0_ref, dk0_ref, dv0_ref, dq1_ref, dk1_ref, dv1_ref,
     dq2_ref, dk2_ref, dv2_ref, dscr, dscr2) = rest[nwd:]

    def wd_cols(col):
        return wd_refs[col // W_BLOCK][:, col % W_BLOCK:col % W_BLOCK + MXU_N]

    tm = ROW_TILE
    x = x_ref[...]
    h = (x * _rms_scale(x, D_MODEL) * ng_ref[...]).astype(BF16)

    nd = DIL_GROUPS * DIL_WIDTH
    bd = bd_ref[...]
    outs = ((dq0_ref, dk0_ref, dv0_ref), (dq1_ref, dk1_ref, dv1_ref), (dq2_ref, dk2_ref, dv2_ref))
    n_store = [0]

    def store(kind, c, val):
        g, half = divmod(c, DIL_WIDTH // MXU_N)
        sl = slice(MXU_N * half, MXU_N * (half + 1))
        d = DIL_PATTERNS[g][1]
        ref = outs[g][kind]
        if d == 1:
            ref[:, sl] = val.astype(BF16)
        else:
            slot = n_store[0] % N_DSCR
            n_store[0] += 1
            for cb in range(MXU_N // LANES):
                dscr[slot, cb] = val[:, LANES * cb:LANES * (cb + 1)]
            for cb in range(MXU_N // LANES):
                c0 = MXU_N * half + LANES * cb
                if d <= RELAYOUT_STRIDE:
                    for r in range(d):
                        ref[r, :, c0:c0 + LANES] = dscr[slot, cb, pl.ds(r, tm // d, stride=d), :].astype(BF16)
                else:
                    f = RELAYOUT_STRIDE
                    for r0 in range(f):
                        dscr2[slot, cb, r0 * (tm // f):(r0 + 1) * (tm // f), :] = (
                            dscr[slot, cb, pl.ds(r0, tm // f, stride=f), :])
                    for r0 in range(f):
                        for r1 in range(d // f):
                            ref[f * r1 + r0, :, c0:c0 + LANES] = dscr2[
                                slot, cb, pl.ds(r0 * (tm // f) + r1, tm // d, stride=d // f), :].astype(BF16)

    def finish(kind, c, t):
        g_ref = gdq_ref if kind == 0 else gdk_ref
        sl = slice(MXU_N * c, MXU_N * (c + 1))
        ssq = _dot((t * t).astype(BF16), bd)
        store(kind, c, t * lax.rsqrt(ssq * (1.0 / DIL_HEAD_DIM) + EPS) * g_ref[:, sl])

    pending = []
    nch = nd // MXU_N

    def dil_chunk(kind, c):
        t = _dot(h, wd_cols(kind * nd + MXU_N * c))
        if len(pending) >= PREP_SSQ_LAG:
            finish(*pending.pop(0))
        pending.append((kind, c, t))
        if kind == 1:
            store(2, c, _dot(h, wd_cols(2 * nd + MXU_N * c)))

    chunks = [(kind, c) for kind in range(2) for c in range(nch)]

    def dil_steps(n):
        for _ in range(n):
            if chunks:
                dil_chunk(*chunks.pop(0))


    cq = _dot(h, wcq_ref[...])
    lat = _dot(h, jnp.concatenate([wckv_ref[...], wkpe_ref[...]], axis=1))
    dil_steps(PREP_INTERLEAVE[0])
    ckv = lat[:, :MLA_KV_LORA]
    kpe = lat[:, MLA_KV_LORA:]
    cqn = (cq * _rms_scale(cq, MLA_Q_LORA) * gqa_ref[...]).astype(BF16)
    ckvn = (ckv * _rms_scale(ckv, MLA_KV_LORA) * gkva_ref[...]).astype(BF16)

    qraw = _dot(cqn, wuq_ref[...])
    kn = _dot(ckvn, wk_ref[...])
    vp = _dot(ckvn, wv_ref[...]) + vone_ref[...]
    v_ref[...] = vp.astype(BF16)
    dil_steps(PREP_INTERLEAVE[1])

    lane = lax.broadcasted_iota(jnp.int32, (1, LANES), 1)
    first_half = lane < MLA_NOPE + MLA_ROPE // 2

    def swap_halves(t):
        return jnp.where(first_half, pltpu.roll(t, LANES - MLA_ROPE // 2, 1),
                         pltpu.roll(t, MLA_ROPE // 2, 1))

    gcq = gcq_ref[...]
    gsq = gsq_ref[...]
    gck = gck_ref[...]
    krot = swap_halves(kpe) * gsk_ref[...]
    for hd in range(MLA_HEADS):
        sl = slice(HEAD_PAD * hd, HEAD_PAD * (hd + 1))
        qh = qraw[:, sl]
        q_ref[:, sl] = (_rms_scale(qh, MLA_QK) * (qh * gcq + swap_halves(qh) * gsq)).astype(BF16)
        kh = kn[:, sl] + kpe
        k_ref[:, sl] = (_rms_scale(kh, MLA_QK) * (kh * gck + krot)).astype(BF16)
        dil_steps(PREP_INTERLEAVE[2])

    dil_steps(len(chunks))
    while pending:
        finish(*pending.pop(0))


def _w_spec(l, width, idx):
    return pl.BlockSpec((None, D_MODEL, width), lambda *_: (l, 0, idx), pipeline_mode=pl.Buffered(1))


def _prep_call(x2, pw, wfirst, wrest, l, B):
    T = x2.shape[0]
    tm = ROW_TILE
    nst = SEQ // tm
    row = lambda n: pl.BlockSpec((tm, n), lambda i: (i, 0))
    tab = pl.BlockSpec((tm, LANES), lambda i: (i % nst, 0))
    consts = [pw['ng']]
    wspecs = [_w_spec(l, MLA_Q_LORA, _O_CQ // MLA_Q_LORA), _w_spec(l, MLA_KV_LORA, _O_CKV // MLA_KV_LORA)]
    consts1 = [pw['wkpe'], pw['gqa'], pw['gkva'], pw['wuq'], pw['wk'], pw['wv'], pw['vone']]
    tabs = [pw['gcq'], pw['gsq'], pw['gck'], pw['gsk']]
    consts2 = [pw['gdq'], pw['gdk'], pw['bd']]
    wd_blocks = range((_O_DQ - _O_BZ) // W_BLOCK, (_O_CZ - _O_BZ) // W_BLOCK)
    in_specs = ([row(D_MODEL)] + [_const_spec(a.shape) for a in consts] + wspecs
                + [_const_spec(a.shape) for a in consts1] + [tab] * 4
                + [_const_spec(a.shape) for a in consts2]
                + [_w_spec(l, W_BLOCK, b) for b in wd_blocks])
    args = ([x2] + consts + [wfirst, wfirst] + consts1 + tabs + consts2 + [wrest] * len(wd_blocks))
    out_shape = [jax.ShapeDtypeStruct((T, MLA_HEADS * HEAD_PAD), BF16)] * 3
    out_specs = [row(MLA_HEADS * HEAD_PAD)] * 3
    for _, d in DIL_PATTERNS:
        if d == 1:
            out_shape += [jax.ShapeDtypeStruct((T, DIL_WIDTH), BF16)] * 3
            out_specs += [row(DIL_WIDTH)] * 3
        else:
            out_shape += [jax.ShapeDtypeStruct((B, d, SEQ // d, DIL_WIDTH), BF16)] * 3
            out_specs += [pl.BlockSpec((None, d, tm // d, DIL_WIDTH),
                                       lambda i: (i // nst, 0, i % nst, 0))] * 3
    return pl.pallas_call(
        _prep_kernel,
        grid=(T // tm,),
        in_specs=in_specs,
        out_specs=out_specs,
        out_shape=out_shape,
        scratch_shapes=[pltpu.VMEM((N_DSCR, MXU_N // LANES, tm, LANES), F32),
                        pltpu.VMEM((N_DSCR, MXU_N // LANES, tm, LANES), F32)],
        compiler_params=pltpu.CompilerParams(dimension_semantics=("arbitrary",),
                                             vmem_limit_bytes=VMEM_LIMIT),
        name="prep",
    )(*args)


def _mla_kernel(q_ref, k_ref, v_ref, o_ref, acc_ref, m_ref):
    t = MLA_T
    nt = SEQ // t
    hq = t // 2
    lane = lax.broadcasted_iota(jnp.int32, (1, LANES), 1)
    lo = lane < HALF
    top_mask = (lax.broadcasted_iota(jnp.int32, (hq, hq), 1)
                <= lax.broadcasted_iota(jnp.int32, (hq, hq), 0))
    bot_mask = (lax.broadcasted_iota(jnp.int32, (hq, t), 1)
                <= lax.broadcasted_iota(jnp.int32, (hq, t), 0) + hq)

    updates = []
    for j in range(nt):
        for r0 in range((j + 1) * t, SEQ, MLA_TALL):
            for hd in range(2):
                updates.append((hd, r0, min(MLA_TALL, SEQ - r0), j * t, t, None, j == 0, False))
        for r0, nr, nk, mask in ((j * t, hq, hq, top_mask), (j * t + hq, hq, t, bot_mask)):
            for hd in range(2):
                updates.append((hd, r0, nr, j * t, nk, mask, j == 0, True))

    def scores(hd, r0, nr, k0, nk, mask, first, last):
        sl = slice(HEAD_PAD * hd, HEAD_PAD * (hd + 1))
        s = _dot_t(q_ref[r0:r0 + nr, sl], k_ref[k0:k0 + nk, sl])
        return s if mask is None else jnp.where(mask, s, NEG)

    def softmax(s, hd, r0, nr, k0, nk, mask, first, last):
        m_new = jnp.broadcast_to(jnp.max(s, axis=-1, keepdims=True), (nr, LANES))
        m_old = None
        if not first:
            m_old = m_ref[hd, r0:r0 + nr, :]
            m_new = jnp.maximum(m_old, m_new)
        p = jnp.exp2(s - jnp.concatenate([m_new] * (nk // LANES), axis=1)).astype(BF16)
        return m_old, m_new, p

    def weighted_values(p, hd, r0, nr, k0, nk, mask, first, last):
        return _dot(p, v_ref[k0:k0 + nk, HEAD_PAD * hd:HEAD_PAD * (hd + 1)])

    def finish(m_old, m_new, acc, hd, r0, nr, k0, nk, mask, first, last):
        if not first:
            acc = acc_ref[hd, r0:r0 + nr, :] * jnp.exp2(m_old - m_new) + acc
        if last:
            return acc / pltpu.roll(acc, HALF, 1)
        acc_ref[hd, r0:r0 + nr, :] = acc
        m_ref[hd, r0:r0 + nr, :] = m_new
        return None

    nu = len(updates)
    l1, l2, l3 = np.cumsum(MLA_LAG)
    written_at = {}
    ss, mp, pv, done = {}, {}, {}, {}
    for step in range(nu + l3):
        if step < nu:
            ss[step] = scores(*updates[step])
        u = step - l1
        if 0 <= u < nu:
            hd, r0, nr = updates[u][:3]
            for rr in range(r0, r0 + nr, hq):
                assert updates[u][6] or written_at[(hd, rr)] < step, "softmax state read too early"
            mp[u] = softmax(ss.pop(u), *updates[u])
        u = step - l2
        if 0 <= u < nu:
            pv[u] = weighted_values(mp[u][2], *updates[u])
        u = step - l3
        if 0 <= u < nu:
            hd, r0, nr = updates[u][:3]
            m_old, m_new, _ = mp.pop(u)
            done[u] = finish(m_old, m_new, pv.pop(u), *updates[u])
            for rr in range(r0, r0 + nr, hq):
                written_at[(hd, rr)] = step
            if updates[u][7] and hd == 1:
                o_ref[r0:r0 + nr, :] = jnp.where(lo, done.pop(u - 1), done.pop(u)).astype(BF16)


def _mla_call(q, k, v, B):
    T = q.shape[0]
    blk = pl.BlockSpec((SEQ, 2 * HEAD_PAD), lambda b, hp: (b, hp))
    return pl.pallas_call(
        _mla_kernel,
        grid=(B, MLA_HEADS // 2),
        in_specs=[blk, blk, blk],
        out_specs=pl.BlockSpec((SEQ, LANES), lambda b, hp: (b, hp)),
        out_shape=jax.ShapeDtypeStruct((T, MLA_HEADS * MLA_V), BF16),
        scratch_shapes=[pltpu.VMEM((2, SEQ, LANES), F32),
                        pltpu.VMEM((2, SEQ, LANES), F32)],
        compiler_params=pltpu.CompilerParams(dimension_semantics=("arbitrary", "arbitrary"),
                                             vmem_limit_bytes=VMEM_LIMIT),
        name="mla",
    )(q, k, v)


def _dil_kernel(bias_ref, q0_ref, k0_ref, v0_ref, q1_ref, k1_ref, v1_ref, q2_ref, k2_ref, v2_ref,
                o_ref, og_ref, lse_ref, nat_ref):
    lane = lax.broadcasted_iota(jnp.int32, (1, LANES), 1)
    lo = lane < HALF
    Q = Q_BLOCK
    ins = ((q0_ref, k0_ref, v0_ref), (q1_ref, k1_ref, v1_ref), (q2_ref, k2_ref, v2_ref))
    ones = jnp.ones((2 * Q, LANES), BF16)

    def rows_of(ref, g, r, a, b):
        return ref[a:b, :] if DIL_PATTERNS[g][1] == 1 else ref[r, a:b, :]

    def key_window(n):
        k0 = max(n - 1, 0) * Q
        return k0, (n + 1) * Q - k0

    def scores(g, r, n):
        k0, nk = key_window(n)
        q = rows_of(ins[g][0], g, r, n * Q, (n + 1) * Q)
        zero = jnp.zeros_like(q)
        qb = jnp.concatenate([jnp.where(lo, q, zero), jnp.where(lo, zero, q)], axis=0)
        return _dot_t(qb, rows_of(ins[g][1], g, r, k0, k0 + nk)) + bias_ref[g, :, 2 * Q - nk:]

    def softmax(s):
        m = jnp.max(s, axis=-1, keepdims=True)
        return m, jnp.exp2(s - m).astype(BF16)

    def weighted_values(g, r, n, p):
        k0, nk = key_window(n)
        vb = jnp.concatenate([rows_of(ins[g][2], g, r, k0, k0 + nk), ones[:nk]], axis=1)
        return _dot(p, vb)

    def finish(g, r, n, m, acc):
        d = DIL_PATTERNS[g][1]
        mb = jnp.broadcast_to(m, (2 * Q, LANES))
        den = jnp.where(lo, acc[:Q, LANES:], acc[Q:, LANES:])
        if d == 1:
            rows = slice(n * Q, (n + 1) * Q)
        elif d == MERGE_D:
            rows = slice(r * (SEQ // d) + n * Q, r * (SEQ // d) + (n + 1) * Q)
        else:
            rows = pl.ds((r % MERGE_D) * (SEQ // MERGE_D) + r // MERGE_D, Q, stride=MERGE_D)
        og_ref[g, rows, :] = jnp.where(lo, acc[:Q, :LANES], acc[Q:, :LANES]) / den
        lse_ref[g, rows, :] = jnp.where(lo, mb[:Q], mb[Q:]) + jnp.log2(den)

    blocks = [(g, r, n) for g, (_, d) in enumerate(DIL_PATTERNS)
              for r in range(d) for n in range(SEQ // d // Q)]
    nb = len(blocks)
    ss, mp, accs = {}, {}, {}
    l1, l2, l3 = np.cumsum(DIL_LAG)
    for step in range(nb + l3):
        if step < nb:
            ss[step] = scores(*blocks[step])
        b = step - l1
        if 0 <= b < nb:
            mp[b] = softmax(ss.pop(b))
        b = step - l2
        if 0 <= b < nb:
            accs[b] = weighted_values(*blocks[b], mp[b][1])
        b = step - l3
        if 0 <= b < nb:
            finish(*blocks[b], mp.pop(b)[0], accs.pop(b))

    chunk = 256
    per_class = SEQ // MERGE_D
    for c in range(SEQ // chunk):
        r0, i0 = divmod(c * chunk, per_class)
        cm = slice(c * chunk, (c + 1) * chunk)
        nat = pl.ds(MERGE_D * i0 + r0, chunk, stride=MERGE_D)
        rows = (nat, cm, cm)
        ls = [lse_ref[g, rows[g], :] for g in range(DIL_GROUPS)]
        mx = jnp.maximum(jnp.maximum(ls[0], ls[1]), ls[2])
        num = jnp.zeros((chunk, LANES), F32)
        den = jnp.zeros((chunk, LANES), F32)
        for g in range(DIL_GROUPS):
            w = jnp.exp2(ls[g] - mx)
            num = num + w * og_ref[g, rows[g], :]
            den = den + w
        nat_ref[nat, :] = num / den
    o_ref[...] = nat_ref[...].astype(BF16)


def _dil_call(bias, dil_in, B):
    T = B * SEQ
    nhp = DIL_HEADS // 2
    in_specs = [pl.BlockSpec((DIL_GROUPS, None, 2 * Q_BLOCK, 2 * Q_BLOCK), lambda b, hp: (0, hp, 0, 0))]
    for _, d in DIL_PATTERNS:
        if d == 1:
            spec = pl.BlockSpec((SEQ, LANES), lambda b, hp: (b, hp))
        else:
            spec = pl.BlockSpec((None, d, SEQ // d, LANES), lambda b, hp: (b, 0, 0, hp))
        in_specs += [spec] * 3
    return pl.pallas_call(
        _dil_kernel,
        grid=(B, nhp),
        in_specs=in_specs,
        out_specs=pl.BlockSpec((SEQ, LANES), lambda b, hp: (b, hp)),
        out_shape=jax.ShapeDtypeStruct((T, DIL_WIDTH), BF16),
        scratch_shapes=[pltpu.VMEM((DIL_GROUPS, SEQ, LANES), F32),
                        pltpu.VMEM((DIL_GROUPS, SEQ, LANES), F32),
                        pltpu.VMEM((SEQ, LANES), F32)],
        compiler_params=pltpu.CompilerParams(dimension_semantics=("arbitrary", "arbitrary"),
                                             vmem_limit_bytes=VMEM_LIMIT),
        name="dil",
    )(bias, *dil_in)


def _out_kernel(x_ref, ob_ref, oc_ref, ng_ref, wa_ref, wbz_ref, wcz_ref,
                wg0_ref, wg1_ref, wg2_ref, wg3_ref, wg4_ref, wg5_ref, bg_ref, cw_ref, cb_ref,
                woa_ref, wob_ref, woc_ref, wo_ref, out_ref, ubuf):
    wg_refs = (wg0_ref, wg1_ref, wg2_ref, wg3_ref, wg4_ref, wg5_ref)

    def gate_pre(b):
        return jnp.concatenate([_dot(h, wg_refs[2 * b][...]), _dot(h, wg_refs[2 * b + 1][...])], axis=1)

    tm = ROW_TILE
    i = pl.program_id(0)
    @pl.when(i % (SEQ // tm) == 0)
    def _():
        ubuf[0:8, :] = jnp.zeros((8, CONV_WIDTH), F32)

    W = CONV_WIDTH
    hs, us = [], []
    for pc in range(HEAD_PIECES):
        xp = x_ref[pc * (tm // HEAD_PIECES):(pc + 1) * (tm // HEAD_PIECES), :]
        hp = (xp * _rms_scale(xp, D_MODEL) * ng_ref[...]).astype(BF16)
        hs.append(hp)
        us.append(_dot(hp, wa_ref[:, W:2 * W]))
    h = jnp.concatenate(hs, axis=0)
    x = x_ref[...]

    u = jnp.concatenate(us, axis=0) * _dot(h, wa_ref[:, 2 * W:3 * W])
    ubuf[8:8 + tm, :] = u
    conv = (cb_ref[...] + ubuf[6:6 + tm, :] * cw_ref[0:1, :] + ubuf[7:7 + tm, :] * cw_ref[1:2, :]
            + u * cw_ref[2:3, :])
    ubuf[0:8, :] = u[tm - 8:tm, :]
    az = _dot(h, wa_ref[:, 3 * W:4 * W])
    ya = _dot(h, wa_ref[:, 0:W]) * conv * (az * _sigmoid(az))

    D = D_MODEL
    ga = _sigmoid(gate_pre(0) + bg_ref[:, 0:D])
    merged = ga * _dot(ya.astype(BF16), woa_ref[...])

    zb = _dot(h, wbz_ref[...])
    yb = ob_ref[...].astype(F32) * (zb * _sigmoid(zb))
    gb = _sigmoid(gate_pre(1) + bg_ref[:, D:2 * D])
    merged = merged + gb * _dot(yb.astype(BF16), wob_ref[...])

    zc = _dot(h, wcz_ref[...])
    yc = oc_ref[...].astype(F32) * (zc * _sigmoid(zc))
    gc = _sigmoid(gate_pre(2) + bg_ref[:, 2 * D:3 * D])
    merged = merged + gc * _dot(yc.astype(BF16), woc_ref[...])

    out_ref[...] = x + _dot(merged.astype(BF16), wo_ref[...])


def _out_call(x2, ob, oc, ow, wfirst, wrest, l):
    T = x2.shape[0]
    tm = ROW_TILE
    row = lambda n: pl.BlockSpec((tm, n), lambda i: (i, 0))
    consts = [ow['bg'], ow['cw'], ow['cb'], ow['woa'], ow['wob'], ow['woc'], ow['wo']]
    rb = lambda off: (off - _O_BZ) // W_BLOCK
    gate_blocks = range(rb(_O_GATE), rb(_O_END))
    wspecs = ([_w_spec(l, 4 * CONV_WIDTH, 0), _w_spec(l, W_BLOCK, rb(_O_BZ)), _w_spec(l, W_BLOCK, rb(_O_CZ))]
              + [_w_spec(l, W_BLOCK, b) for b in gate_blocks])
    wargs = [wfirst] + [wrest] * (2 + len(gate_blocks))
    return pl.pallas_call(
        _out_kernel,
        grid=(T // tm,),
        in_specs=[row(D_MODEL), row(MLA_HEADS * MLA_V), row(DIL_WIDTH), _const_spec(ow['ng'].shape)]
                 + wspecs + [_const_spec(a.shape) for a in consts],
        out_specs=row(D_MODEL),
        out_shape=jax.ShapeDtypeStruct((T, D_MODEL), F32),
        scratch_shapes=[pltpu.VMEM((tm + 8, CONV_WIDTH), F32)],
        compiler_params=pltpu.CompilerParams(dimension_semantics=("arbitrary",),
                                             vmem_limit_bytes=VMEM_LIMIT),
        name="out",
    )(x2, ob, oc, ow['ng'], *wargs, *consts)


def _rope_tables():
    inv = ROPE_THETA ** (-jnp.arange(0, MLA_ROPE, 2, dtype=F32) / MLA_ROPE)
    ang = jnp.arange(SEQ, dtype=F32)[:, None] * inv[None, :]
    return jnp.cos(ang), jnp.sin(ang)


def _head_tables(g, cos, sin, scale):
    hr = MLA_ROPE // 2
    g1, g2 = g[MLA_NOPE:MLA_NOPE + hr], g[MLA_NOPE + hr:MLA_QK]
    ones = jnp.ones((SEQ, 1), F32)
    zpad = jnp.zeros((SEQ, HEAD_PAD - MLA_QK), F32)
    gc = jnp.concatenate([ones * g[None, :MLA_NOPE], cos * g1[None], cos * g2[None], zpad], axis=1)
    gs = jnp.concatenate([jnp.zeros((SEQ, MLA_NOPE), F32), -sin * g2[None], sin * g1[None], zpad], axis=1)
    return gc * scale, gs * scale


def _dil_bias():
    n = DIL_GROUPS * DIL_HEADS
    slopes = (2.0 ** (-8.0 * jnp.arange(1, n + 1, dtype=F32) / n)).reshape(DIL_GROUPS, DIL_HEADS)
    qq = jnp.arange(Q_BLOCK)[:, None]
    kk = jnp.arange(2 * Q_BLOCK)[None, :]
    j = Q_BLOCK + qq - kk
    tabs = []
    for gi, (window, d) in enumerate(DIL_PATTERNS):
        valid = (j >= 0) & (j <= window // d)
        dist = (d * j).astype(F32)
        tabs.append(jnp.where(valid[None], (-LOG2E) * slopes[gi][:, None, None] * dist[None], NEG))
    tab = jnp.stack(tabs, axis=0)
    return tab.reshape(DIL_GROUPS, DIL_HEADS // 2, 2 * Q_BLOCK, 2 * Q_BLOCK)


def _layer_params(l, p, cos, sin):
    zc = lambda n: jnp.zeros((D_MODEL, n), F32)
    wkpe = jnp.concatenate([zc(MLA_NOPE), p['w_in'][l, :, _O_KPE:_O_BZ], zc(HEAD_PAD - MLA_QK)], axis=1)

    wuq = p['w_uq'][l].reshape(MLA_Q_LORA, MLA_HEADS, MLA_QK)
    zq = jnp.zeros((MLA_Q_LORA, MLA_HEADS, HEAD_PAD - MLA_QK), F32)
    wuq_p = jnp.concatenate([wuq, zq], axis=-1).reshape(MLA_Q_LORA, MLA_HEADS * HEAD_PAD)

    wukv = p['w_ukv'][l].reshape(MLA_KV_LORA, MLA_HEADS, MLA_NOPE + MLA_V)
    zk = jnp.zeros((MLA_KV_LORA, MLA_HEADS, HALF), F32)
    wk = jnp.concatenate([wukv[..., :MLA_NOPE], zk], axis=-1).reshape(MLA_KV_LORA, MLA_HEADS * HEAD_PAD)
    wv4 = wukv[..., MLA_NOPE:].reshape(MLA_KV_LORA, MLA_HEADS // 2, 2, MLA_V)
    zv = jnp.zeros((MLA_KV_LORA, MLA_HEADS // 2, MLA_V), F32)
    wv = jnp.stack([jnp.concatenate([wv4[:, :, 0], zv], axis=-1),
                    jnp.concatenate([zv, wv4[:, :, 1]], axis=-1)], axis=2)
    wv = wv.reshape(MLA_KV_LORA, MLA_HEADS * HEAD_PAD)
    half_pat = jnp.concatenate([jnp.zeros((HALF,), F32), jnp.ones((HALF,), F32)])
    vone = jnp.concatenate([half_pat, 1.0 - half_pat] * (MLA_HEADS // 2))[None, :]

    gcq, gsq = _head_tables(p['mla_q_norm_g'][l], cos, sin, MLA_QK ** -0.5 * LOG2E)
    gck, gsk = _head_tables(p['mla_k_norm_g'][l], cos, sin, 1.0)

    gdq = (jnp.tile(p['dil_q_norm_g'][l][:, None, :], (1, DIL_HEADS, 1)).reshape(1, -1)
           * (DIL_HEAD_DIM ** -0.5 * LOG2E))
    gdk = jnp.tile(p['dil_k_norm_g'][l][:, None, :], (1, DIL_HEADS, 1)).reshape(1, -1)
    ii = jnp.arange(MXU_N) // DIL_HEAD_DIM
    bd = (ii[:, None] == ii[None, :]).astype(BF16)

    ng = p['norm_g'][l][None, :]
    prep = dict(ng=ng, wkpe=wkpe.astype(BF16), gqa=p['q_a_norm_g'][l][None, :],
                gkva=p['kv_a_norm_g'][l][None, :], wuq=wuq_p.astype(BF16),
                wk=wk.astype(BF16), wv=wv.astype(BF16), vone=vone, gcq=gcq, gsq=gsq, gck=gck, gsk=gsk,
                gdq=gdq, gdk=gdk, bd=bd)
    out = dict(ng=ng, bg=p['b_gate'][l][None, :],
               cw=p['conv_w'][l], cb=p['conv_b'][l][None, :],
               woa=p['w_out_a'][l].astype(BF16), wob=p['w_out_b'][l].astype(BF16),
               woc=p['w_out_c'][l].astype(BF16), wo=p['w_o'][l].astype(BF16))
    return prep, out


def kernel(x, norm_g, w_in, b_gate, conv_w, conv_b, q_a_norm_g, w_uq, kv_a_norm_g, w_ukv, mla_q_norm_g,
           mla_k_norm_g, dil_q_norm_g, dil_k_norm_g, w_out_a, w_out_b, w_out_c, w_o):
    B, S, D = x.shape
    assert S == SEQ and D == D_MODEL
    T = B * S
    p = dict(norm_g=norm_g, w_in=w_in, b_gate=b_gate, conv_w=conv_w, conv_b=conv_b,
             q_a_norm_g=q_a_norm_g, w_uq=w_uq, kv_a_norm_g=kv_a_norm_g, w_ukv=w_ukv,
             mla_q_norm_g=mla_q_norm_g, mla_k_norm_g=mla_k_norm_g, dil_q_norm_g=dil_q_norm_g,
             dil_k_norm_g=dil_k_norm_g, w_out_a=w_out_a, w_out_b=w_out_b, w_out_c=w_out_c, w_o=w_o)
    cos, sin = _rope_tables()
    bias = _dil_bias()
    wfirst = w_in[:, :, :_O_BZ].astype(BF16)
    wrest = w_in[:, :, _O_BZ:].astype(BF16)
    x2 = x.reshape(T, D)
    for l in range(DEPTH):
        pw, ow = _layer_params(l, p, cos, sin)
        q, k, v, *dil_in = _prep_call(x2, pw, wfirst, wrest, l, B)
        ob = _mla_call(q, k, v, B)
        oc = _dil_call(bias, dil_in, B)
        x2 = _out_call(x2, ob, oc, ow, wfirst, wrest, l)
    return x2.reshape(B, S, D)
```
